```python
import math
import jax, jax.numpy as jnp
from jax import lax
import numpy as np

D_MODEL = 1024
BATCH = 32
SEQ = 2048
DEPTH = 4

D_LRU = 1024
LRU_HEADS = 16
LRU_HEAD_DIM = D_LRU // LRU_HEADS
CONV_WIDTH = 4
LRU_C = 8.0
D_POOL = 512
POOL_WINDOWS = (2, 4, 8, 16)
POOL_GROUPS = len(POOL_WINDOWS)
POOL_GROUP_DIM = D_POOL // POOL_GROUPS
D_IN = 2 * D_LRU + D_POOL + 2 * D_MODEL
N_GROUPS = 4
EXPERTS_PER_GROUP = 8
N_EXPERTS = N_GROUPS * EXPERTS_PER_GROUP
TOP_K_EXPERT = 2
D_EXPERT = 512
EXPERT_BLOCK = 256
PLE_DIM = 256
EPS = 1e-6

kernel_name = "hybrid_rglru_pool_hmoe_ple"


def rms_norm(x, g):
    xf = x.astype(jnp.float32)
    y = xf * lax.rsqrt(jnp.mean(xf * xf, axis=-1, keepdims=True) + EPS)
    return (y * g.astype(jnp.float32)).astype(x.dtype)


def causal_depthwise_conv(x, w, b):
    c = x.shape[-1]
    y = lax.conv_general_dilated(x, w[:, None, :].astype(x.dtype), window_strides=(1,),
                                 padding=[(CONV_WIDTH - 1, 0)],
                                 dimension_numbers=('NWC', 'WIO', 'NWC'),
                                 feature_group_count=c)
    return y + b.astype(x.dtype)


def _lin_rec_combine(c1, c2):
    a1, b1 = c1
    a2, b2 = c2
    return a1 * a2, a2 * b1 + b2


def rg_lru(x, w_r, b_r, w_i, b_i, lam):
    bsz, s, _ = x.shape
    xf = x.astype(jnp.float32)
    xh = xf.reshape(bsz, s, LRU_HEADS, LRU_HEAD_DIM)
    r = jax.nn.sigmoid(jnp.einsum('bshi,hij->bshj', xh, w_r.astype(jnp.float32)).reshape(bsz, s, D_LRU) + b_r.astype(jnp.float32))
    i = jax.nn.sigmoid(jnp.einsum('bshi,hij->bshj', xh, w_i.astype(jnp.float32)).reshape(bsz, s, D_LRU) + b_i.astype(jnp.float32))
    log_a = -LRU_C * r * jax.nn.softplus(-lam.astype(jnp.float32))
    a = jnp.exp(log_a)
    mult = jnp.sqrt(-jnp.expm1(2.0 * log_a))
    u = mult * (i * xf)
    _, h = lax.associative_scan(_lin_rec_combine, (a, u), axis=1)
    return h.astype(x.dtype)


def multiscale_pool(x, w_pool, scale):
    bsz, s, _ = x.shape
    xg = x.astype(jnp.float32).reshape(bsz, s, POOL_GROUPS, POOL_GROUP_DIM)
    csum = jnp.cumsum(xg, axis=1)
    pos = jnp.arange(s, dtype=jnp.float32)
    outs = []
    for g, w in enumerate(POOL_WINDOWS):
        c = csum[:, :, g]
        shifted = jnp.pad(c, ((0, 0), (w, 0), (0, 0)))[:, :s]
        count = jnp.minimum(pos + 1.0, float(w))[None, :, None]
        outs.append((c - shifted) / count - xg[:, :, g])
    pooled = jnp.stack(outs, axis=2)
    mixed = jnp.einsum('bsgc,gcd->bsgd', pooled, w_pool.astype(jnp.float32)).reshape(bsz, s, D_POOL)
    return (mixed * scale.astype(jnp.float32)).astype(x.dtype)


def hierarchical_moe(u, w_rg, b_rg, w_re, b_re, w_gate_e, w_up_e, w_down_e):
    n, d = u.shape
    uf = u.astype(jnp.float32)
    group_logits = uf @ w_rg.astype(jnp.float32) + b_rg.astype(jnp.float32)
    g_idx = jnp.argmax(group_logits, axis=-1)
    p_group = jnp.take_along_axis(jax.nn.softmax(group_logits, axis=-1), g_idx[:, None], axis=-1)
    exp_logits = (uf @ w_re.astype(jnp.float32) + b_re.astype(jnp.float32)).reshape(n, N_GROUPS, EXPERTS_PER_GROUP)
    sel = jnp.take_along_axis(exp_logits, g_idx[:, None, None], axis=1)[:, 0]
    top_v, top_i = lax.top_k(sel, TOP_K_EXPERT)
    gates = p_group * jax.nn.softmax(top_v, axis=-1)
    expert_ids = g_idx[:, None].astype(jnp.int32) * EXPERTS_PER_GROUP + top_i.astype(jnp.int32)

    a_tot = n * TOP_K_EXPERT
    flat_e = expert_ids.reshape(-1)
    flat_tok = jnp.repeat(jnp.arange(n, dtype=jnp.int32), TOP_K_EXPERT)
    flat_w = gates.reshape(-1)
    order = jnp.argsort(flat_e)
    e_s = flat_e[order]
    tok_s = flat_tok[order]
    w_s = flat_w[order]
    counts = jnp.bincount(flat_e, length=N_EXPERTS)
    padded = ((counts + EXPERT_BLOCK - 1) // EXPERT_BLOCK) * EXPERT_BLOCK
    pad_end = jnp.cumsum(padded)
    pad_start = pad_end - padded
    start = jnp.cumsum(counts) - counts
    dest = pad_start[e_s] + (jnp.arange(a_tot, dtype=jnp.int32) - start[e_s])
    num_blocks = (a_tot + N_EXPERTS * (EXPERT_BLOCK - 1) + EXPERT_BLOCK - 1) // EXPERT_BLOCK
    rows = num_blocks * EXPERT_BLOCK
    row_tok = jnp.full((rows,), n, dtype=jnp.int32).at[dest].set(tok_s)
    u_pad = jnp.concatenate([u, jnp.zeros((1, d), u.dtype)], axis=0)
    xb = u_pad[row_tok].reshape(num_blocks, EXPERT_BLOCK, d)
    block_expert = jnp.minimum(
        jnp.searchsorted(pad_end, jnp.arange(num_blocks, dtype=jnp.int32) * EXPERT_BLOCK, side='right'),
        N_EXPERTS - 1)

    def expert_block(args):
        xblk, e = args
        h = jax.nn.silu(xblk @ w_gate_e[e]) * (xblk @ w_up_e[e])
        return h @ w_down_e[e]

    y = lax.map(expert_block, (xb, block_expert)).reshape(rows, d)
    contrib = y[dest] * w_s[:, None].astype(y.dtype)
    return jax.ops.segment_sum(contrib, tok_s, num_segments=n)


def setup_inputs(seed: int = 0) -> dict:
    key = jax.random.key(seed)
    ks = jax.random.split(key, 32)
    L, D = DEPTH, D_MODEL
    nrm = lambda k, shape, scale: jax.random.normal(k, shape, jnp.float32) * scale
    a_lo, a_hi = 0.9 ** (1.0 / LRU_C), 0.999 ** (1.0 / LRU_C)
    a0 = jax.random.uniform(ks[9], (L, D_LRU), jnp.float32, a_lo, a_hi)
    return {
        "x": nrm(ks[0], (BATCH, SEQ, D), 1.0),
        "p": nrm(ks[1], (DEPTH, BATCH, SEQ, PLE_DIM), 1.0),
        "norm_mix_g": 1.0 + nrm(ks[2], (L, D), 0.05),
        "w_in": nrm(ks[3], (L, D, D_IN), D ** -0.5),
        "conv_w": nrm(ks[4], (L, CONV_WIDTH, D_LRU), CONV_WIDTH ** -0.5),
        "conv_b": nrm(ks[5], (L, D_LRU), 0.02),
        "w_rg": nrm(ks[6], (L, LRU_HEADS, LRU_HEAD_DIM, LRU_HEAD_DIM), LRU_HEAD_DIM ** -0.5),
        "b_rg": nrm(ks[7], (L, D_LRU), 0.02),
        "w_ig": nrm(ks[8], (L, LRU_HEADS, LRU_HEAD_DIM, LRU_HEAD_DIM), LRU_HEAD_DIM ** -0.5),
        "b_ig": nrm(ks[10], (L, D_LRU), 0.02),
        "lru_lambda": jnp.log(a0) - jnp.log1p(-a0),
        "w_pool": nrm(ks[11], (L, POOL_GROUPS, POOL_GROUP_DIM, POOL_GROUP_DIM), POOL_GROUP_DIM ** -0.5),
        "pool_scale": 1.0 + nrm(ks[12], (L, D_POOL), 0.05),
        "w_up_a": nrm(ks[13], (L, D_LRU, D), D_LRU ** -0.5),
        "w_up_b": nrm(ks[14], (L, D_POOL, D), D_POOL ** -0.5),
        "w_out": nrm(ks[15], (L, D, D), D ** -0.5),
        "norm_ffn_g": 1.0 + nrm(ks[16], (L, D), 0.05),
        "w_router_group": nrm(ks[17], (L, D, N_GROUPS), D ** -0.5),
        "b_router_group": nrm(ks[18], (L, N_GROUPS), 0.01),
        "w_router_expert": nrm(ks[19], (L, D, N_EXPERTS), D ** -0.5),
        "b_router_expert": nrm(ks[20], (L, N_EXPERTS), 0.01),
        "w_gate_e": nrm(ks[21], (L, N_EXPERTS, D, D_EXPERT), D ** -0.5),
        "w_up_e": nrm(ks[22], (L, N_EXPERTS, D, D_EXPERT), D ** -0.5),
        "w_down_e": nrm(ks[23], (L, N_EXPERTS, D_EXPERT, D), D_EXPERT ** -0.5),
        "norm_ple_g": 1.0 + nrm(ks[24], (L, D), 0.05),
        "w_ple": nrm(ks[25], (L, PLE_DIM, D), PLE_DIM ** -0.5),
        "w_ple_gate": nrm(ks[26], (L, D, D), D ** -0.5),
        "final_norm_g": 1.0 + nrm(ks[27], (D,), 0.05),
    }


def reference(x, p, norm_mix_g, w_in, conv_w, conv_b, w_rg, b_rg, w_ig, b_ig, lru_lambda,
              w_pool, pool_scale, w_up_a, w_up_b, w_out, norm_ffn_g,
              w_router_group, b_router_group, w_router_expert, b_router_expert,
              w_gate_e, w_up_e, w_down_e, norm_ple_g, w_ple, w_ple_gate, final_norm_g):
    bsz, s, d = x.shape
    h = x
    for l in range(DEPTH):
        u = rms_norm(h, norm_mix_g[l])
        z = u @ w_in[l]
        xa = z[..., :D_LRU]
        ya = z[..., D_LRU:2 * D_LRU]
        xb = z[..., 2 * D_LRU:2 * D_LRU + D_POOL]
        gate_logits = z[..., 2 * D_LRU + D_POOL:]
        xa = causal_depthwise_conv(xa, conv_w[l], conv_b[l])
        ha = rg_lru(xa, w_rg[l], b_rg[l], w_ig[l], b_ig[l], lru_lambda[l]) * jax.nn.gelu(ya)
        hb = multiscale_pool(xb, w_pool[l], pool_scale[l])
        gates = jax.nn.sigmoid(gate_logits.astype(jnp.float32)).astype(h.dtype)
        merged = gates[..., :d] * (ha @ w_up_a[l]) + gates[..., d:] * (hb @ w_up_b[l])
        h = h + merged @ w_out[l]
        u = rms_norm(h, norm_ffn_g[l])
        moe_out = hierarchical_moe(u.reshape(bsz * s, d), w_router_group[l], b_router_group[l],
                                   w_router_expert[l], b_router_expert[l],
                                   w_gate_e[l], w_up_e[l], w_down_e[l])
        h = h + moe_out.reshape(bsz, s, d)
        u = rms_norm(h, norm_ple_g[l])
        ple_gate = jax.nn.sigmoid((u @ w_ple_gate[l]).astype(jnp.float32)).astype(h.dtype)
        h = h + ple_gate * (p[l] @ w_ple[l])
    return rms_norm(h, final_norm_g)
```

```python
import functools

import jax
import jax.numpy as jnp
from jax import lax
from jax.experimental import pallas as pl
from jax.experimental.pallas import tpu as pltpu

EPS = 1e-6
LRU_C = 8.0
EXPERT_BLOCK = 256
TOP_K = 2
GELU_C0 = 0.7978845608028654
GELU_C1 = 0.044715

LANES = 128
SUBLANES = 8
HALO = 16

F32 = jnp.float32
BF16 = jnp.bfloat16
U32 = jnp.uint32
MIB = 1024 * 1024


def _dot(a, b):
    return jnp.dot(a, b, preferred_element_type=F32)


def _sigmoid(x):
    return 1.0 / (1.0 + jnp.exp(-x))


def _rms_norm(x, g):
    return x * lax.rsqrt(jnp.mean(x * x, axis=-1, keepdims=True) + EPS) * g


def _gelu_tanh(x):
    return 0.5 * x * (1.0 + jnp.tanh(GELU_C0 * (x + GELU_C1 * (x * x * x))))


def _softplus(x):
    return jnp.maximum(x, 0.0) + jnp.log1p(jnp.exp(-jnp.abs(x)))


def _pack_bf16_pairs(x):
    half = x.shape[1] // 2
    bits = lax.bitcast_convert_type(x.astype(BF16).astype(F32), U32)
    return (bits[:, :half] >> 16) | (bits[:, half:] & jnp.uint32(0xFFFF0000))


def _unpack_bf16_pairs(w):
    lo = lax.bitcast_convert_type(w << 16, F32).astype(BF16)
    hi = lax.bitcast_convert_type(w & jnp.uint32(0xFFFF0000), F32).astype(BF16)
    return lo, hi


def _linear_scan(a, b):
    t_len = a.shape[0]
    row = lax.broadcasted_iota(jnp.int32, a.shape, 0)
    k = 1
    while k < t_len:
        if k < SUBLANES:
            keep = row >= k
            a_prev = jnp.where(keep, pltpu.roll(a, k, 0), 1.0)
            b_prev = jnp.where(keep, pltpu.roll(b, k, 0), 0.0)
            b = a * b_prev + b
            a = a * a_prev
        else:
            b = jnp.concatenate([b[:k], a[k:] * b[:-k] + b[k:]], axis=0)
            a = jnp.concatenate([a[:k], a[k:] * a[:-k]], axis=0)
        k *= 2
    return a, b


def _mixer_kernel(h_ref, g1_ref, win_ref, convw_ref, convb_ref, wgate_ref, brg_ref, big_ref, lam_ref,
                  wpool_ref, pscale_ref, wupa_ref, wupb_ref, wout_ref, g2_ref, wr_ref, br_ref,
                  hmid_ref, u2_ref, route_ref,
                  xa_ext, xb_ext, hstate, ha_scr, hb_scr,
                  *, tc, d_model, d_lru, d_pool, n_groups, experts_per_group):
    c = pl.program_id(1)

    @pl.when(c == 0)
    def _():
        xa_ext[0:HALO, :] = jnp.zeros((HALO, d_lru), F32)
        xb_ext[0:HALO, :] = jnp.zeros((HALO, d_pool), F32)
        hstate[...] = jnp.zeros_like(hstate)

    x = h_ref[...]
    u = _rms_norm(x, g1_ref[...]).astype(BF16)
    o_pool = 2 * d_lru
    o_gate = o_pool + d_pool
    xa_pre = _dot(u, win_ref[:, 0:d_lru])
    ya = _dot(u, win_ref[:, d_lru:o_pool])
    xb = _dot(u, win_ref[:, o_pool:o_gate])
    gl_a = _dot(u, win_ref[:, o_gate:o_gate + d_model])
    gl_b = _dot(u, win_ref[:, o_gate + d_model:o_gate + 2 * d_model])

    conv_k = convw_ref.shape[0]
    xa_ext[HALO:HALO + tc, :] = xa_pre
    xa = convb_ref[...] + convw_ref[conv_k - 1:conv_k, :] * xa_pre
    for j in range(1, conv_k):
        xa = xa + convw_ref[conv_k - 1 - j:conv_k - j, :] * xa_ext[pl.ds(HALO - j, tc), :]
    xa_ext[0:HALO, :] = xa_ext[tc:tc + HALO, :]

    for p in range(d_lru // LANES):
        sl = slice(p * LANES, (p + 1) * LANES)
        xa_s = xa[:, sl]
        rg = _dot(xa_s.astype(BF16), wgate_ref[p])
        r = _sigmoid(rg[:, :LANES] + brg_ref[:, sl])
        i = _sigmoid(rg[:, LANES:] + big_ref[:, sl])
        log_a = r * (-LRU_C * _softplus(-lam_ref[:, sl]))
        a = jnp.exp(log_a)
        b = jnp.sqrt(1.0 - a * a) * (i * xa_s)
        a_cum, h_loc = _linear_scan(a, b)
        hseq = h_loc + a_cum * hstate[:, sl]
        hstate[:, sl] = hseq[tc - 1:tc, :]
        ha_scr[:, sl] = (hseq * _gelu_tanh(ya[:, sl])).astype(BF16)

    xb_ext[HALO:HALO + tc, :] = xb
    t_glob = (c * tc + lax.broadcasted_iota(jnp.int32, (tc, LANES), 0)).astype(F32)
    for g in range(d_pool // LANES):
        sl = slice(g * LANES, (g + 1) * LANES)
        window = 2 << g
        xb_s = xb[:, sl]
        s = xb_s
        for j in range(1, window):
            s = s + xb_ext[pl.ds(HALO - j, tc), sl]
        pooled = s / jnp.minimum(t_glob + 1.0, float(window)) - xb_s
        mixed = _dot(pooled.astype(BF16), wpool_ref[g]) * pscale_ref[:, sl]
        hb_scr[:, sl] = mixed.astype(BF16)
    xb_ext[0:HALO, :] = xb_ext[tc:tc + HALO, :]

    merged = _sigmoid(gl_a) * _dot(ha_scr[...], wupa_ref[...]) + _sigmoid(gl_b) * _dot(hb_scr[...], wupb_ref[...])
    h_new = x + _dot(merged.astype(BF16), wout_ref[...])
    hmid_ref[...] = h_new

    u2 = _rms_norm(h_new, g2_ref[...])
    u2_ref[...] = _pack_bf16_pairs(u2)
    logits = _dot(u2.astype(BF16), wr_ref[...]) + br_ref[...]
    lane = lax.broadcasted_iota(jnp.int32, (tc, LANES), 1).astype(F32)
    neg = -jnp.inf
    big = float(LANES)
    is_group = lane < n_groups
    glog = jnp.where(is_group, logits, neg)
    gmax = jnp.max(glog, axis=-1, keepdims=True)
    gidx = jnp.min(jnp.where(glog == gmax, lane, big), axis=-1, keepdims=True)
    psum = jnp.sum(jnp.where(is_group, jnp.exp(glog - gmax), 0.0), axis=-1, keepdims=True)
    lo = n_groups + gidx * experts_per_group
    in_group = (lane >= lo) & (lane < lo + experts_per_group)
    sel = jnp.where(in_group, logits, neg)
    m1 = jnp.max(sel, axis=-1, keepdims=True)
    i1 = jnp.min(jnp.where(sel == m1, lane, big), axis=-1, keepdims=True)
    sel2 = jnp.where(lane == i1, neg, sel)
    m2 = jnp.max(sel2, axis=-1, keepdims=True)
    i2 = jnp.min(jnp.where(sel2 == m2, lane, big), axis=-1, keepdims=True)
    e21 = jnp.exp(m2 - m1)
    w1 = 1.0 / (psum * (1.0 + e21))
    w2 = e21 * w1
    route = jnp.where(lane == 0, i1 - n_groups,
                      jnp.where(lane == 1, i2 - n_groups,
                                jnp.where(lane == 2, w1, jnp.where(lane == 3, w2, 0.0))))
    route_ref[...] = route


def _rank_kernel(route_ref, rank_ref, counts_ref, tri_scr, run_scr, *, tr):
    i = pl.program_id(0)

    @pl.when(i == 0)
    def _():
        r_i = lax.broadcasted_iota(jnp.int32, (tr, tr), 0)
        c_i = lax.broadcasted_iota(jnp.int32, (tr, tr), 1)
        tri_scr[...] = jnp.where(r_i > c_i, 1.0, 0.0).astype(BF16)
        run_scr[...] = jnp.zeros_like(run_scr)

    r = route_ref[...]
    lane = lax.broadcasted_iota(jnp.int32, (tr, LANES), 1).astype(F32)
    oh0 = lane == r[:, 0:1]
    oh1 = lane == r[:, 1:2]
    oh = jnp.where(oh0 | oh1, 1.0, 0.0)
    before = _dot(tri_scr[...], oh.astype(BF16)) + run_scr[0:1, :]
    rk0 = jnp.sum(jnp.where(oh0, before, 0.0), axis=-1, keepdims=True)
    rk1 = jnp.sum(jnp.where(oh1, before, 0.0), axis=-1, keepdims=True)
    rank_ref[...] = jnp.where(lane == 0, rk0, jnp.where(lane == 1, rk1, 0.0))
    run_scr[...] = run_scr[...] + jnp.sum(oh, axis=0, keepdims=True)
    counts_ref[...] = run_scr[...]


def _dispatch_kernel(pos_ref, u2_ref, xs_in_ref, xs_ref, sem, *, td):
    del xs_in_ref

    def issue(t, carry):
        for k in range(TOP_K):
            dst = pos_ref[0, TOP_K * t + k]
            pltpu.make_async_copy(u2_ref.at[pl.ds(t, 1)], xs_ref.at[pl.ds(dst, 1)], sem).start()
        return carry

    lax.fori_loop(0, td, issue, 0)
    for _ in range(TOP_K):
        pltpu.make_async_copy(u2_ref, xs_ref.at[pl.ds(0, td)], sem).wait()


def _expert_kernel(be_ref, xs_ref, wg_ref, wu_ref, wd_ref, ys_ref):
    del be_ref
    half = xs_ref.shape[1]
    lo, hi = _unpack_bf16_pairs(xs_ref[...])
    gate = _dot(lo, wg_ref[0:half, :]) + _dot(hi, wg_ref[half:, :])
    up = _dot(lo, wu_ref[0:half, :]) + _dot(hi, wu_ref[half:, :])
    act = (gate * _sigmoid(gate)) * up
    ys_ref[...] = _dot(act.astype(BF16), wd_ref[...])


def _combine_kernel(pos_ref, hmid_ref, route_ref, p_ref, g3_ref, wple_ref, wpg_ref, gf_ref, ys_ref,
                    out_ref, ybuf, sem, *, tp, final):
    def issue(t, carry):
        for k in range(TOP_K):
            src = pos_ref[0, TOP_K * t + k]
            pltpu.make_async_copy(ys_ref.at[pl.ds(src, 1)], ybuf.at[k, pl.ds(t, 1)], sem).start()
        return carry

    lax.fori_loop(0, tp, issue, 0)
    for k in range(TOP_K):
        pltpu.make_async_copy(ys_ref.at[pl.ds(0, tp)], ybuf.at[k], sem).wait()

    r = route_ref[...]
    h2 = hmid_ref[...] + r[:, 2:3] * ybuf[0] + r[:, 3:4] * ybuf[1]
    u3 = _rms_norm(h2, g3_ref[...]).astype(BF16)
    gate = _sigmoid(_dot(u3, wpg_ref[...]))
    h3 = h2 + gate * _dot(p_ref[...].astype(BF16), wple_ref[...])
    if final:
        h3 = _rms_norm(h3, gf_ref[...])
    out_ref[...] = h3


def _const_spec(shape, layer, n_grid):
    nd = len(shape)
    if n_grid == 1:
        index_map = lambda i: (layer,) + (0,) * nd
    else:
        index_map = lambda i, j: (layer,) + (0,) * nd
    return pl.BlockSpec((None,) + tuple(shape), index_map, pipeline_mode=pl.Buffered(1))


def _block_diag_gates(w_r, w_i):
    n_layers, n_heads, hd, _ = w_r.shape
    per = LANES // hd
    n_slab = n_heads // per

    def bd(w):
        w = w.reshape(n_layers, n_slab, per, hd, hd)
        eye = jnp.eye(per, dtype=w.dtype)
        out = jnp.einsum('lspij,pq->lspiqj', w, eye)
        return out.reshape(n_layers, n_slab, LANES, LANES)

    return jnp.concatenate([bd(w_r), bd(w_i)], axis=-1).astype(BF16)


def kernel(x, p, norm_mix_g, w_in, conv_w, conv_b, w_rg, b_rg, w_ig, b_ig, lru_lambda, w_pool, pool_scale, w_up_a, w_up_b, w_out, norm_ffn_g, w_router_group, b_router_group, w_router_expert, b_router_expert, w_gate_e, w_up_e, w_down_e, norm_ple_g, w_ple, w_ple_gate, final_norm_g):
    bsz, seq, d_model = x.shape
    n_layers, _, d_in = w_in.shape
    d_lru = conv_w.shape[2]
    d_pool = pool_scale.shape[1]
    n_groups = w_router_group.shape[2]
    n_experts = w_router_expert.shape[2]
    experts_per_group = n_experts // n_groups
    d_expert = w_gate_e.shape[3]
    ple_dim = w_ple.shape[1]
    n_tok = bsz * seq
    half = d_model // 2

    tc = min(256, seq)
    tr = min(512, n_tok)
    td = min(512, n_tok)
    tp = min(256, n_tok)
    assert seq % tc == 0 and n_tok % tr == 0 and n_tok % td == 0 and n_tok % tp == 0
    assert d_in == 2 * d_lru + d_pool + 2 * d_model
    assert LANES % w_rg.shape[2] == 0 and d_lru % LANES == 0 and d_model % (2 * LANES) == 0
    assert w_pool.shape[2] == LANES and (2 << (w_pool.shape[1] - 1)) <= HALO and conv_w.shape[1] - 1 <= HALO
    assert n_groups + n_experts <= LANES

    n_blocks = (n_tok * TOP_K + n_experts * (EXPERT_BLOCK - 1) + EXPERT_BLOCK - 1) // EXPERT_BLOCK
    n_rows = n_blocks * EXPERT_BLOCK

    w_in_b = w_in.astype(BF16)
    w_gate_bd = _block_diag_gates(w_rg, w_ig)
    w_pool_b = w_pool.astype(BF16)
    w_up_a_b = w_up_a.astype(BF16)
    w_up_b_b = w_up_b.astype(BF16)
    w_out_b = w_out.astype(BF16)
    w_router = jnp.concatenate([w_router_group, w_router_expert], axis=-1)
    w_router = jnp.pad(w_router, ((0, 0), (0, 0), (0, LANES - w_router.shape[-1]))).astype(BF16)
    b_router = jnp.concatenate([b_router_group, b_router_expert], axis=-1)
    b_router = jnp.pad(b_router, ((0, 0), (0, LANES - b_router.shape[-1])))[:, None, :]
    w_gate_b = w_gate_e.astype(BF16)
    w_up_b2 = w_up_e.astype(BF16)
    w_down_b = w_down_e.astype(BF16)
    w_ple_b = w_ple.astype(BF16)
    w_pg_b = w_ple_gate.astype(BF16)
    row3 = lambda a: a[:, None, :]
    g1, g2, g3 = row3(norm_mix_g), row3(norm_ffn_g), row3(norm_ple_g)
    conv_b3, b_rg3, b_ig3, lam3, pscale3 = row3(conv_b), row3(b_rg), row3(b_ig), row3(lru_lambda), row3(pool_scale)
    gf = final_norm_g[None, :]
    p2 = p.reshape(n_layers, n_tok, ple_dim)

    h = x.reshape(n_tok, d_model)
    n_chunks = seq // tc

    for layer in range(n_layers):
        cs2 = functools.partial(_const_spec, layer=layer, n_grid=2)
        tok_map2 = lambda b, c: (b * n_chunks + c, 0)
        h_mid, u2, route = pl.pallas_call(
            functools.partial(_mixer_kernel, tc=tc, d_model=d_model, d_lru=d_lru, d_pool=d_pool,
                              n_groups=n_groups, experts_per_group=experts_per_group),
            grid=(bsz, n_chunks),
            in_specs=[
                pl.BlockSpec((tc, d_model), tok_map2),
                cs2((1, d_model)), cs2((d_model, d_in)), cs2(conv_w.shape[1:]), cs2((1, d_lru)),
                cs2(w_gate_bd.shape[1:]), cs2((1, d_lru)), cs2((1, d_lru)), cs2((1, d_lru)),
                cs2(w_pool.shape[1:]), cs2((1, d_pool)), cs2((d_lru, d_model)), cs2((d_pool, d_model)),
                cs2((d_model, d_model)), cs2((1, d_model)), cs2((d_model, LANES)), cs2((1, LANES)),
            ],
            out_specs=[
                pl.BlockSpec((tc, d_model), tok_map2),
                pl.BlockSpec((tc, half), tok_map2),
                pl.BlockSpec((tc, LANES), tok_map2),
            ],
            out_shape=[
                jax.ShapeDtypeStruct((n_tok, d_model), F32),
                jax.ShapeDtypeStruct((n_tok, half), U32),
                jax.ShapeDtypeStruct((n_tok, LANES), F32),
            ],
            scratch_shapes=[
                pltpu.VMEM((HALO + tc, d_lru), F32),
                pltpu.VMEM((HALO + tc, d_pool), F32),
                pltpu.VMEM((1, d_lru), F32),
                pltpu.VMEM((tc, d_lru), BF16),
                pltpu.VMEM((tc, d_pool), BF16),
            ],
            compiler_params=pltpu.CompilerParams(
                dimension_semantics=("arbitrary", "arbitrary"), vmem_limit_bytes=48 * MIB),
            name=f"mixer_l{layer}",
        )(h, g1, w_in_b, conv_w, conv_b3, w_gate_bd, b_rg3, b_ig3, lam3, w_pool_b, pscale3,
          w_up_a_b, w_up_b_b, w_out_b, g2, w_router, b_router)

        rank, counts = pl.pallas_call(
            functools.partial(_rank_kernel, tr=tr),
            grid=(n_tok // tr,),
            in_specs=[pl.BlockSpec((tr, LANES), lambda i: (i, 0))],
            out_specs=[pl.BlockSpec((tr, LANES), lambda i: (i, 0)),
                       pl.BlockSpec((SUBLANES, LANES), lambda i: (0, 0))],
            out_shape=[jax.ShapeDtypeStruct((n_tok, LANES), F32),
                       jax.ShapeDtypeStruct((SUBLANES, LANES), F32)],
            scratch_shapes=[pltpu.VMEM((tr, tr), BF16), pltpu.VMEM((SUBLANES, LANES), F32)],
            compiler_params=pltpu.CompilerParams(dimension_semantics=("arbitrary",)),
            name=f"rank_l{layer}",
        )(route)

        cnt = counts[0, :n_experts].astype(jnp.int32)
        padded = ((cnt + EXPERT_BLOCK - 1) // EXPERT_BLOCK) * EXPERT_BLOCK
        pad_end = jnp.cumsum(padded)
        pad_start = pad_end - padded
        block_expert = jnp.minimum(
            jnp.searchsorted(pad_end, jnp.arange(n_blocks, dtype=jnp.int32) * EXPERT_BLOCK, side='right'),
            n_experts - 1).astype(jnp.int32)
        eid = route[:, :TOP_K].astype(jnp.int32)
        pos = pad_start[eid] + rank[:, :TOP_K].astype(jnp.int32)

        xs = pl.pallas_call(
            functools.partial(_dispatch_kernel, td=td),
            grid=(n_tok // td,),
            in_specs=[
                pl.BlockSpec((None, 1, TOP_K * td), lambda i: (i, 0, 0), memory_space=pltpu.SMEM),
                pl.BlockSpec((td, half), lambda i: (i, 0)),
                pl.BlockSpec(memory_space=pl.ANY),
            ],
            out_specs=pl.BlockSpec(memory_space=pl.ANY),
            out_shape=jax.ShapeDtypeStruct((n_rows, half), U32),
            scratch_shapes=[pltpu.SemaphoreType.DMA],
            input_output_aliases={2: 0},
            compiler_params=pltpu.CompilerParams(dimension_semantics=("arbitrary",), has_side_effects=True),
            name=f"dispatch_l{layer}",
        )(pos.reshape(n_tok // td, 1, TOP_K * td), u2, jnp.zeros((n_rows, half), U32))

        ys = pl.pallas_call(
            _expert_kernel,
            grid_spec=pltpu.PrefetchScalarGridSpec(
                num_scalar_prefetch=1,
                grid=(n_blocks,),
                in_specs=[
                    pl.BlockSpec((EXPERT_BLOCK, half), lambda j, be: (j, 0)),
                    pl.BlockSpec((None, None, d_model, d_expert), lambda j, be: (layer, be[j], 0, 0)),
                    pl.BlockSpec((None, None, d_model, d_expert), lambda j, be: (layer, be[j], 0, 0)),
                    pl.BlockSpec((None, None, d_expert, d_model), lambda j, be: (layer, be[j], 0, 0)),
                ],
                out_specs=pl.BlockSpec((EXPERT_BLOCK, d_model), lambda j, be: (j, 0)),
            ),
            out_shape=jax.ShapeDtypeStruct((n_rows, d_model), F32),
            compiler_params=pltpu.CompilerParams(dimension_semantics=("arbitrary",)),
            name=f"experts_l{layer}",
        )(block_expert, xs, w_gate_b, w_up_b2, w_down_b)

        final = layer == n_layers - 1
        cs1 = functools.partial(_const_spec, layer=layer, n_grid=1)
        h = pl.pallas_call(
            functools.partial(_combine_kernel, tp=tp, final=final),
            grid=(n_tok // tp,),
            in_specs=[
                pl.BlockSpec((None, 1, TOP_K * tp), lambda i: (i, 0, 0), memory_space=pltpu.SMEM),
                pl.BlockSpec((tp, d_model), lambda i: (i, 0)),
                pl.BlockSpec((tp, LANES), lambda i: (i, 0)),
                pl.BlockSpec((None, tp, ple_dim), lambda i: (layer, i, 0)),
                cs1((1, d_model)), cs1((ple_dim, d_model)), cs1((d_model, d_model)),
                pl.BlockSpec((1, d_model), lambda i: (0, 0)),
                pl.BlockSpec(memory_space=pl.ANY),
            ],
            out_specs=pl.BlockSpec((tp, d_model), lambda i: (i, 0)),
            out_shape=jax.ShapeDtypeStruct((n_tok, d_model), F32),
            scratch_shapes=[pltpu.VMEM((TOP_K, tp, d_model), F32), pltpu.SemaphoreType.DMA],
            compiler_params=pltpu.CompilerParams(dimension_semantics=("arbitrary",)),
            name=f"combine_l{layer}",
        )(pos.reshape(n_tok // tp, 1, TOP_K * tp), h_mid, route, p2, g3, w_ple_b, w_pg_b, gf, ys)

    return h.reshape(bsz, seq, d_model)
```

```python
import functools

import jax
import jax.numpy as jnp
from jax import lax
from jax.experimental import pallas as pl
from jax.experimental.pallas import tpu as pltpu

EPS = 1e-6
LRU_C = 8.0
EXPERT_BLOCK = 256
TOP_K = 2
GELU_C0 = 0.7978845608028654
GELU_C1 = 0.044715

LANES = 128
SUBLANES = 8
SEQ_GROUP = SUBLANES
HALO_STEPS = 16
HALO = HALO_STEPS * SEQ_GROUP
DMA_UNROLL = 8

F32 = jnp.float32
BF16 = jnp.bfloat16
U32 = jnp.uint32
I32 = jnp.int32
MIB = 1024 * 1024


def _dot(a, b):
    return jnp.dot(a, b, preferred_element_type=F32)


def _sigmoid(x):
    return 1.0 / (1.0 + jnp.exp(-x))


def _rms_norm(x, g):
    return x * lax.rsqrt(jnp.mean(x * x, axis=-1, keepdims=True) + EPS) * g


def _gelu_tanh(x):
    return 0.5 * x * (1.0 + jnp.tanh(GELU_C0 * (x + GELU_C1 * (x * x * x))))


def _softplus(x):
    return jnp.maximum(x, 0.0) + jnp.log1p(jnp.exp(-jnp.abs(x)))


def _sqrt_nonneg(x):
    return jnp.where(x > 0.0, x * lax.rsqrt(x), 0.0)


def _pack_bf16_pairs(x):
    half = x.shape[1] // 2
    bits = lax.bitcast_convert_type(x.astype(BF16).astype(F32), U32)
    return (bits[:, :half] >> 16) | (bits[:, half:] & jnp.uint32(0xFFFF0000))


def _unpack_bf16_pairs(w):
    lo = lax.bitcast_convert_type(w << 16, F32).astype(BF16)
    hi = lax.bitcast_convert_type(w & jnp.uint32(0xFFFF0000), F32).astype(BF16)
    return lo, hi


def _mixer_kernel(h_ref, g1_ref, win_ref, convw_ref, convb_ref, wgate_ref, brg_ref, big_ref, lam_ref,
                  wpool_ref, pscale_ref, wupa_ref, wupb_ref, wout_ref, g2_ref, wr_ref, br_ref,
                  hmid_ref, u2_ref, route_ref,
                  xa_ext, xb_ext, hstate, ha_scr, hb_scr,
                  *, rows, d_model, d_lru, d_pool, n_groups, experts_per_group):
    c = pl.program_id(1)
    steps = rows // SEQ_GROUP

    @pl.when(c == 0)
    def _():
        xa_ext[0:HALO, :] = jnp.zeros((HALO, d_lru), F32)
        xb_ext[0:HALO, :] = jnp.zeros((HALO, d_pool), F32)
        hstate[...] = jnp.zeros_like(hstate)

    x = h_ref[...]
    u = _rms_norm(x, g1_ref[...]).astype(BF16)
    o_pool = 2 * d_lru
    o_gate = o_pool + d_pool
    xa_pre = _dot(u, win_ref[:, 0:d_lru])
    ya = _dot(u, win_ref[:, d_lru:o_pool])
    xb = _dot(u, win_ref[:, o_pool:o_gate])
    gl_a = _dot(u, win_ref[:, o_gate:o_gate + d_model])
    gl_b = _dot(u, win_ref[:, o_gate + d_model:o_gate + 2 * d_model])

    conv_k = convw_ref.shape[0]
    xa_ext[HALO:HALO + rows, :] = xa_pre
    xa = convb_ref[...] + convw_ref[conv_k - 1:conv_k, :] * xa_pre
    for j in range(1, conv_k):
        xa = xa + convw_ref[conv_k - 1 - j:conv_k - j, :] * xa_ext[HALO - j * SEQ_GROUP:HALO - j * SEQ_GROUP + rows, :]
    xa_ext[0:HALO, :] = xa_ext[rows:rows + HALO, :]

    for p in range(d_lru // LANES):
        sl = slice(p * LANES, (p + 1) * LANES)
        xa_s = xa[:, sl]
        rg = _dot(xa_s.astype(BF16), wgate_ref[p])
        r = _sigmoid(rg[:, :LANES] + brg_ref[:, sl])
        i = _sigmoid(rg[:, LANES:] + big_ref[:, sl])
        log_a = r * (-LRU_C * _softplus(-lam_ref[:, sl]))
        a = jnp.exp(log_a)
        b = _sqrt_nonneg(1.0 - a * a) * (i * xa_s)
        h = hstate[:, sl]
        hs = []
        for t in range(steps):
            rs = slice(t * SEQ_GROUP, (t + 1) * SEQ_GROUP)
            h = a[rs] * h + b[rs]
            hs.append(h)
        hstate[:, sl] = h
        hseq = jnp.concatenate(hs, axis=0)
        ha_scr[:, sl] = (hseq * _gelu_tanh(ya[:, sl])).astype(BF16)

    xb_ext[HALO:HALO + rows, :] = xb
    t_glob = (c * steps + lax.broadcasted_iota(I32, (rows, LANES), 0) // SEQ_GROUP).astype(F32)
    for g in range(d_pool // LANES):
        sl = slice(g * LANES, (g + 1) * LANES)
        window = 2 << g
        s = xb_ext[:, sl]
        span = 1
        while span < window:
            s = s[span * SEQ_GROUP:] + s[:-span * SEQ_GROUP]
            span *= 2
        s = s[s.shape[0] - rows:]
        xb_s = xb[:, sl]
        pooled = s / jnp.minimum(t_glob + 1.0, float(window)) - xb_s
        mixed = _dot(pooled.astype(BF16), wpool_ref[g]) * pscale_ref[:, sl]
        hb_scr[:, sl] = mixed.astype(BF16)
    xb_ext[0:HALO, :] = xb_ext[rows:rows + HALO, :]

    merged = _sigmoid(gl_a) * _dot(ha_scr[...], wupa_ref[...]) + _sigmoid(gl_b) * _dot(hb_scr[...], wupb_ref[...])
    h_new = x + _dot(merged.astype(BF16), wout_ref[...])
    hmid_ref[...] = h_new

    u2 = _rms_norm(h_new, g2_ref[...])
    u2_ref[...] = _pack_bf16_pairs(u2)
    logits = _dot(u2.astype(BF16), wr_ref[...]) + br_ref[...]
    lane = lax.broadcasted_iota(I32, (rows, LANES), 1).astype(F32)
    neg = -jnp.inf
    big = float(LANES)
    is_group = lane < n_groups
    glog = jnp.where(is_group, logits, neg)
    gmax = jnp.max(glog, axis=-1, keepdims=True)
    gidx = jnp.min(jnp.where(glog == gmax, lane, big), axis=-1, keepdims=True)
    psum = jnp.sum(jnp.where(is_group, jnp.exp(glog - gmax), 0.0), axis=-1, keepdims=True)
    lo = n_groups + gidx * experts_per_group
    in_group = (lane >= lo) & (lane < lo + experts_per_group)
    sel = jnp.where(in_group, logits, neg)
    m1 = jnp.max(sel, axis=-1, keepdims=True)
    i1 = jnp.min(jnp.where(sel == m1, lane, big), axis=-1, keepdims=True)
    sel2 = jnp.where(lane == i1, neg, sel)
    m2 = jnp.max(sel2, axis=-1, keepdims=True)
    i2 = jnp.min(jnp.where(sel2 == m2, lane, big), axis=-1, keepdims=True)
    e21 = jnp.exp(m2 - m1)
    w1 = 1.0 / (psum * (1.0 + e21))
    w2 = e21 * w1
    route = jnp.where(lane == 0, i1 - n_groups,
                      jnp.where(lane == 1, i2 - n_groups,
                                jnp.where(lane == 2, w1, jnp.where(lane == 3, w2, 0.0))))
    route_ref[...] = route


def _rank_kernel(route_ref, pos_ref, counts_ref, tri_scr, run_scr, start_scr, *, tr, n_experts):
    ph = pl.program_id(0)
    i = pl.program_id(1)

    @pl.when((ph == 0) & (i == 0))
    def _():
        r_i = lax.broadcasted_iota(I32, (tr, tr), 0)
        c_i = lax.broadcasted_iota(I32, (tr, tr), 1)
        tri_scr[...] = jnp.where(r_i > c_i, 1.0, 0.0).astype(BF16)
        run_scr[...] = jnp.zeros_like(run_scr)

    r = route_ref[...]
    lane = lax.broadcasted_iota(I32, (tr, LANES), 1).astype(F32)
    oh0 = lane == r[:, 0:1]
    oh1 = lane == r[:, 1:2]
    oh = jnp.where(oh0 | oh1, 1.0, 0.0)

    @pl.when(ph == 0)
    def _():
        run_scr[...] = run_scr[...] + jnp.sum(oh, axis=0, keepdims=True)
        counts_ref[...] = run_scr[...]

    @pl.when((ph == 1) & (i == 0))
    def _():
        cnt = run_scr[...].astype(I32)
        padded = ((cnt + (EXPERT_BLOCK - 1)) // EXPERT_BLOCK) * EXPERT_BLOCK
        lane8 = lax.broadcasted_iota(I32, (SUBLANES, LANES), 1)
        incl = padded
        k = 1
        while k < n_experts:
            incl = incl + jnp.where(lane8 >= k, pltpu.roll(incl, k, 1), 0)
            k *= 2
        start_scr[...] = (incl - padded).astype(F32)
        run_scr[...] = jnp.zeros_like(run_scr)

    @pl.when(ph == 1)
    def _():
        before = _dot(tri_scr[...], oh.astype(BF16)) + (run_scr[0:1, :] + start_scr[0:1, :])
        rk0 = jnp.sum(jnp.where(oh0, before, 0.0), axis=-1, keepdims=True)
        rk1 = jnp.sum(jnp.where(oh1, before, 0.0), axis=-1, keepdims=True)
        cols = jnp.where(lane == 0, rk0, jnp.where(lane == 1, rk1, 0.0))
        pos_ref[...] = jnp.transpose(cols)[0:SUBLANES, :].astype(I32)
        run_scr[...] = run_scr[...] + jnp.sum(oh, axis=0, keepdims=True)


def _dispatch_kernel(pos_ref, u2_ref, xs_in_ref, xs_ref, sem, *, td):
    del xs_in_ref

    def issue(q, carry):
        for uu in range(DMA_UNROLL):
            t = q * DMA_UNROLL + uu
            for k in range(TOP_K):
                pltpu.make_async_copy(u2_ref.at[pl.ds(t, 1)], xs_ref.at[pl.ds(pos_ref[k, t], 1)], sem).start()
        return carry

    lax.fori_loop(0, td // DMA_UNROLL, issue, 0)
    for _ in range(TOP_K):
        pltpu.make_async_copy(u2_ref, xs_ref.at[pl.ds(0, td)], sem).wait()


def _expert_kernel(be_ref, xs_ref, wg_ref, wu_ref, wd_ref, ys_ref):
    del be_ref
    half = xs_ref.shape[1]
    lo, hi = _unpack_bf16_pairs(xs_ref[...])
    gate = _dot(lo, wg_ref[0:half, :]) + _dot(hi, wg_ref[half:, :])
    up = _dot(lo, wu_ref[0:half, :]) + _dot(hi, wu_ref[half:, :])
    act = (gate * _sigmoid(gate)) * up
    ys_ref[...] = _dot(act.astype(BF16), wd_ref[...])


def _combine_kernel(pos_cur_ref, pos_nxt_ref, hmid_ref, route_ref, p_ref, g3_ref, wple_ref, wpg_ref, gf_ref,
                    ys_ref, out_ref, ybuf, sems, *, tp, final):
    i = pl.program_id(0)
    n = pl.num_programs(0)
    slot = lax.rem(i, 2)

    def gather_row(pos_ref, s, t, k):
        return pltpu.make_async_copy(ys_ref.at[pl.ds(pos_ref[k, t], 1)], ybuf.at[s, k, pl.ds(t, 1)], sems.at[s])

    def wait_tile(s):
        for k in range(TOP_K):
            pltpu.make_async_copy(ys_ref.at[pl.ds(0, tp)], ybuf.at[s, k], sems.at[s]).wait()

    @pl.when(i == 0)
    def _():
        def body(q, carry):
            for uu in range(DMA_UNROLL):
                for k in range(TOP_K):
                    gather_row(pos_cur_ref, 0, q * DMA_UNROLL + uu, k).start()
            return carry
        lax.fori_loop(0, tp // DMA_UNROLL, body, 0)

    wait_tile(slot)

    for t in range(tp):
        for k in range(TOP_K):
            gather_row(pos_nxt_ref, 1 - slot, t, k).start()

    r = route_ref[...]
    h2 = hmid_ref[...] + r[:, 2:3] * ybuf[slot, 0] + r[:, 3:4] * ybuf[slot, 1]
    u3 = _rms_norm(h2, g3_ref[...]).astype(BF16)
    gate = _sigmoid(_dot(u3, wpg_ref[...]))
    h3 = h2 + gate * _dot(p_ref[...].astype(BF16), wple_ref[...])
    if final:
        h3 = _rms_norm(h3, gf_ref[...])
    out_ref[...] = h3

    @pl.when(i == n - 1)
    def _():
        wait_tile(1 - slot)


def _const_spec(shape, layer, n_grid):
    nd = len(shape)
    if n_grid == 1:
        index_map = lambda i: (layer,) + (0,) * nd
    else:
        index_map = lambda i, j: (layer,) + (0,) * nd
    return pl.BlockSpec((None,) + tuple(shape), index_map, pipeline_mode=pl.Buffered(1))


def _block_diag_gates(w_r, w_i):
    n_layers, n_heads, hd, _ = w_r.shape
    per = LANES // hd
    n_slab = n_heads // per

    def bd(w):
        w = w.reshape(n_layers, n_slab, per, hd, hd)
        eye = jnp.eye(per, dtype=w.dtype)
        out = jnp.einsum('lspij,pq->lspiqj', w, eye)
        return out.reshape(n_layers, n_slab, LANES, LANES)

    return jnp.concatenate([bd(w_r), bd(w_i)], axis=-1).astype(BF16)


def _tile_sizes(seq, n_tok):
    mixer_rows = min(256, seq * SEQ_GROUP)
    rank_rows = min(512, n_tok)
    combine_rows = min(256, n_tok)
    assert (seq * SEQ_GROUP) % mixer_rows == 0 and n_tok % rank_rows == 0 and rank_rows % combine_rows == 0
    assert mixer_rows % SEQ_GROUP == 0 and rank_rows % DMA_UNROLL == 0 and combine_rows % DMA_UNROLL == 0
    return mixer_rows, rank_rows, combine_rows


def kernel(x, p, norm_mix_g, w_in, conv_w, conv_b, w_rg, b_rg, w_ig, b_ig, lru_lambda, w_pool, pool_scale, w_up_a, w_up_b, w_out, norm_ffn_g, w_router_group, b_router_group, w_router_expert, b_router_expert, w_gate_e, w_up_e, w_down_e, norm_ple_g, w_ple, w_ple_gate, final_norm_g):
    bsz, seq, d_model = x.shape
    n_layers, _, d_in = w_in.shape
    d_lru = conv_w.shape[2]
    d_pool = pool_scale.shape[1]
    n_groups = w_router_group.shape[2]
    n_experts = w_router_expert.shape[2]
    experts_per_group = n_experts // n_groups
    d_expert = w_gate_e.shape[3]
    ple_dim = w_ple.shape[1]
    n_tok = bsz * seq
    half = d_model // 2

    rows_m, tr, tp = _tile_sizes(seq, n_tok)
    assert bsz % SEQ_GROUP == 0
    assert d_in == 2 * d_lru + d_pool + 2 * d_model
    assert LANES % w_rg.shape[2] == 0 and d_lru % LANES == 0 and d_model % (2 * LANES) == 0
    assert w_pool.shape[2] == LANES and (2 << (w_pool.shape[1] - 1)) <= HALO_STEPS and conv_w.shape[1] - 1 <= HALO_STEPS
    assert n_groups + n_experts <= LANES

    n_blocks = (n_tok * TOP_K + n_experts * (EXPERT_BLOCK - 1) + EXPERT_BLOCK - 1) // EXPERT_BLOCK
    n_rows = n_blocks * EXPERT_BLOCK

    w_in_b = w_in.astype(BF16)
    w_gate_bd = _block_diag_gates(w_rg, w_ig)
    w_pool_b = w_pool.astype(BF16)
    w_up_a_b = w_up_a.astype(BF16)
    w_up_b_b = w_up_b.astype(BF16)
    w_out_b = w_out.astype(BF16)
    w_router = jnp.concatenate([w_router_group, w_router_expert], axis=-1)
    w_router = jnp.pad(w_router, ((0, 0), (0, 0), (0, LANES - w_router.shape[-1]))).astype(BF16)
    b_router = jnp.concatenate([b_router_group, b_router_expert], axis=-1)
    b_router = jnp.pad(b_router, ((0, 0), (0, LANES - b_router.shape[-1])))[:, None, :]
    w_gate_b = w_gate_e.astype(BF16)
    w_up_b2 = w_up_e.astype(BF16)
    w_down_b = w_down_e.astype(BF16)
    w_ple_b = w_ple.astype(BF16)
    w_pg_b = w_ple_gate.astype(BF16)
    row3 = lambda a: a[:, None, :]
    g1, g2, g3 = row3(norm_mix_g), row3(norm_ffn_g), row3(norm_ple_g)
    conv_b3, b_rg3, b_ig3, lam3, pscale3 = row3(conv_b), row3(b_rg), row3(b_ig), row3(lru_lambda), row3(pool_scale)
    gf = final_norm_g[None, :]

    n_sg = bsz // SEQ_GROUP
    h = x.reshape(n_sg, SEQ_GROUP, seq, d_model).transpose(0, 2, 1, 3).reshape(n_tok, d_model)
    p2 = p.reshape(n_layers, n_sg, SEQ_GROUP, seq, ple_dim).transpose(0, 1, 3, 2, 4).reshape(n_layers, n_tok, ple_dim)
    n_chunks = seq * SEQ_GROUP // rows_m
    n_rt = n_tok // tr
    per_rt = tr // tp
    n_ct = n_tok // tp

    for layer in range(n_layers):
        cs2 = functools.partial(_const_spec, layer=layer, n_grid=2)
        tok_map2 = lambda b, c: (b * n_chunks + c, 0)
        h_mid, u2, route = pl.pallas_call(
            functools.partial(_mixer_kernel, rows=rows_m, d_model=d_model, d_lru=d_lru, d_pool=d_pool,
                              n_groups=n_groups, experts_per_group=experts_per_group),
            grid=(n_sg, n_chunks),
            in_specs=[
                pl.BlockSpec((rows_m, d_model), tok_map2),
                cs2((1, d_model)), cs2((d_model, d_in)), cs2(conv_w.shape[1:]), cs2((1, d_lru)),
                cs2(w_gate_bd.shape[1:]), cs2((1, d_lru)), cs2((1, d_lru)), cs2((1, d_lru)),
                cs2(w_pool.shape[1:]), cs2((1, d_pool)), cs2((d_lru, d_model)), cs2((d_pool, d_model)),
                cs2((d_model, d_model)), cs2((1, d_model)), cs2((d_model, LANES)), cs2((1, LANES)),
            ],
            out_specs=[
                pl.BlockSpec((rows_m, d_model), tok_map2),
                pl.BlockSpec((rows_m, half), tok_map2),
                pl.BlockSpec((rows_m, LANES), tok_map2),
            ],
            out_shape=[
                jax.ShapeDtypeStruct((n_tok, d_model), F32),
                jax.ShapeDtypeStruct((n_tok, half), U32),
                jax.ShapeDtypeStruct((n_tok, LANES), F32),
            ],
            scratch_shapes=[
                pltpu.VMEM((HALO + rows_m, d_lru), F32),
                pltpu.VMEM((HALO + rows_m, d_pool), F32),
                pltpu.VMEM((SEQ_GROUP, d_lru), F32),
                pltpu.VMEM((rows_m, d_lru), BF16),
                pltpu.VMEM((rows_m, d_pool), BF16),
            ],
            compiler_params=pltpu.CompilerParams(
                dimension_semantics=("arbitrary", "arbitrary"), vmem_limit_bytes=48 * MIB),
            name=f"mixer_l{layer}",
        )(h, g1, w_in_b, conv_w, conv_b3, w_gate_bd, b_rg3, b_ig3, lam3, w_pool_b, pscale3,
          w_up_a_b, w_up_b_b, w_out_b, g2, w_router, b_router)

        pos, counts = pl.pallas_call(
            functools.partial(_rank_kernel, tr=tr, n_experts=n_experts),
            grid=(2, n_rt),
            in_specs=[pl.BlockSpec((tr, LANES), lambda ph, i: (i, 0))],
            out_specs=[pl.BlockSpec((None, SUBLANES, tr), lambda ph, i: (i * ph, 0, 0)),
                       pl.BlockSpec((SUBLANES, LANES), lambda ph, i: (0, 0))],
            out_shape=[jax.ShapeDtypeStruct((n_rt, SUBLANES, tr), I32),
                       jax.ShapeDtypeStruct((SUBLANES, LANES), F32)],
            scratch_shapes=[pltpu.VMEM((tr, tr), BF16), pltpu.VMEM((SUBLANES, LANES), F32),
                            pltpu.VMEM((SUBLANES, LANES), F32)],
            compiler_params=pltpu.CompilerParams(dimension_semantics=("arbitrary", "arbitrary")),
            name=f"rank_l{layer}",
        )(route)

        cnt = counts[0, :n_experts].astype(I32)
        pad_end = jnp.cumsum(((cnt + EXPERT_BLOCK - 1) // EXPERT_BLOCK) * EXPERT_BLOCK)
        block_row0 = jnp.arange(n_blocks, dtype=I32) * EXPERT_BLOCK
        block_expert = jnp.minimum(jnp.sum(pad_end[None, :] <= block_row0[:, None], axis=1), n_experts - 1).astype(I32)

        xs = pl.pallas_call(
            functools.partial(_dispatch_kernel, td=tr),
            grid=(n_rt,),
            in_specs=[
                pl.BlockSpec((None, SUBLANES, tr), lambda i: (i, 0, 0), memory_space=pltpu.SMEM),
                pl.BlockSpec((tr, half), lambda i: (i, 0)),
                pl.BlockSpec(memory_space=pl.ANY),
            ],
            out_specs=pl.BlockSpec(memory_space=pl.ANY),
            out_shape=jax.ShapeDtypeStruct((n_rows, half), U32),
            scratch_shapes=[pltpu.SemaphoreType.DMA],
            input_output_aliases={2: 0},
            compiler_params=pltpu.CompilerParams(dimension_semantics=("arbitrary",), has_side_effects=True),
            name=f"dispatch_l{layer}",
        )(pos, u2, jnp.zeros((n_rows, half), U32))

        ys = pl.pallas_call(
            _expert_kernel,
            grid_spec=pltpu.PrefetchScalarGridSpec(
                num_scalar_prefetch=1,
                grid=(n_blocks,),
                in_specs=[
                    pl.BlockSpec((EXPERT_BLOCK, half), lambda j, be: (j, 0)),
                    pl.BlockSpec((None, None, d_model, d_expert), lambda j, be, layer=layer: (layer, be[j], 0, 0)),
                    pl.BlockSpec((None, None, d_model, d_expert), lambda j, be, layer=layer: (layer, be[j], 0, 0)),
                    pl.BlockSpec((None, None, d_expert, d_model), lambda j, be, layer=layer: (layer, be[j], 0, 0)),
                ],
                out_specs=pl.BlockSpec((EXPERT_BLOCK, d_model), lambda j, be: (j, 0)),
            ),
            out_shape=jax.ShapeDtypeStruct((n_rows, d_model), F32),
            compiler_params=pltpu.CompilerParams(dimension_semantics=("arbitrary",)),
            name=f"experts_l{layer}",
        )(block_expert, xs, w_gate_b, w_up_b2, w_down_b)

        final = layer == n_layers - 1
        cs1 = functools.partial(_const_spec, layer=layer, n_grid=1)
        pos_spec = lambda shift: pl.BlockSpec(
            (None, SUBLANES, tp),
            lambda i: (jnp.minimum(i + shift, n_ct - 1) // per_rt, 0, jnp.minimum(i + shift, n_ct - 1) % per_rt),
            memory_space=pltpu.SMEM)
        h = pl.pallas_call(
            functools.partial(_combine_kernel, tp=tp, final=final),
            grid=(n_ct,),
            in_specs=[
                pos_spec(0), pos_spec(1),
                pl.BlockSpec((tp, d_model), lambda i: (i, 0)),
                pl.BlockSpec((tp, LANES), lambda i: (i, 0)),
                pl.BlockSpec((None, tp, ple_dim), lambda i, layer=layer: (layer, i, 0)),
                cs1((1, d_model)), cs1((ple_dim, d_model)), cs1((d_model, d_model)),
                pl.BlockSpec((1, d_model), lambda i: (0, 0)),
                pl.BlockSpec(memory_space=pl.ANY),
            ],
            out_specs=pl.BlockSpec((tp, d_model), lambda i: (i, 0)),
            out_shape=jax.ShapeDtypeStruct((n_tok, d_model), F32),
            scratch_shapes=[pltpu.VMEM((2, TOP_K, tp, d_model), F32), pltpu.SemaphoreType.DMA((2,))],
            compiler_params=pltpu.CompilerParams(dimension_semantics=("arbitrary",)),
            name=f"combine_l{layer}",
        )(pos, pos, h_mid, route, p2, g3, w_ple_b, w_pg_b, gf, ys)

    out = h.reshape(n_sg, seq, SEQ_GROUP, d_model).transpose(0, 2, 1, 3)
    return out.reshape(bsz, seq, d_model)
```

```python
import functools

import jax
import jax.numpy as jnp
from jax import lax
from jax.experimental import pallas as pl
from jax.experimental.pallas import tpu as pltpu

EPS = 1e-6
LRU_C = 8.0
EXPERT_BLOCK = 256
TOP_K = 2
GELU_C0 = 0.7978845608028654
GELU_C1 = 0.044715

LANES = 128
SUBLANES = 8
SEQ_GROUP = SUBLANES
HALO_STEPS = 16
HALO = HALO_STEPS * SEQ_GROUP
DMA_UNROLL = 8

F32 = jnp.float32
BF16 = jnp.bfloat16
U32 = jnp.uint32
I32 = jnp.int32
MIB = 1024 * 1024


def _dot(a, b):
    return jnp.dot(a, b, preferred_element_type=F32)


def _sigmoid(x):
    return 1.0 / (1.0 + jnp.exp(-x))


def _rms_norm(x, g):
    return x * lax.rsqrt(jnp.mean(x * x, axis=-1, keepdims=True) + EPS) * g


def _gelu_tanh(x):
    return 0.5 * x * (1.0 + jnp.tanh(GELU_C0 * (x + GELU_C1 * (x * x * x))))


def _softplus(x):
    return jnp.maximum(x, 0.0) + jnp.log1p(jnp.exp(-jnp.abs(x)))


def _sqrt_nonneg(x):
    return jnp.where(x > 0.0, x * lax.rsqrt(x), 0.0)


def _pack_bf16_pairs(x):
    half = x.shape[1] // 2
    bits = lax.bitcast_convert_type(x.astype(BF16).astype(F32), U32)
    return (bits[:, :half] >> 16) | (bits[:, half:] & jnp.uint32(0xFFFF0000))


def _unpack_bf16_pairs(w):
    lo = lax.bitcast_convert_type(w << 16, F32).astype(BF16)
    hi = lax.bitcast_convert_type(w & jnp.uint32(0xFFFF0000), F32).astype(BF16)
    return lo, hi


def _store_row_slabs(ref, val):
    n = val.shape[1] // LANES
    for c in range(n):
        ref[pl.ds(c, val.shape[0], stride=n), :] = val[:, c * LANES:(c + 1) * LANES]


def _load_row_slabs(ref, row0, n_rows, n):
    return jnp.concatenate([ref[pl.ds(row0 * n + c, n_rows, stride=n), :] for c in range(n)], axis=1)


def _mixer_kernel(u_ref, h_ref, win_ref, convw_ref, convb_ref, wgate_ref, brg_ref, big_ref, lam_ref,
                  wpool_ref, pscale_ref, wupa_ref, wupb_ref, wout_ref, g2_ref, wr_ref, br_ref,
                  hmid_ref, u2_ref, route_ref, counts_ref,
                  z0, z1, xa_ext, xb_ext, hstate, ha_scr, hb_scr, cnt_scr,
                  *, rows, n_chunks, d_model, d_lru, d_pool, n_groups, experts_per_group):
    s = pl.program_id(0)
    steps = rows // SEQ_GROUP
    o_pool = 2 * d_lru
    o_gate = o_pool + d_pool
    c = lax.rem(jnp.maximum(s - 1, 0), n_chunks)

    @pl.when(s == 0)
    def _():
        z1[...] = jnp.zeros_like(z1)

    @pl.when(s <= 1)
    def _():
        cnt_scr[...] = jnp.zeros_like(cnt_scr)

    @pl.when(c == 0)
    def _():
        xa_ext[0:HALO, :] = jnp.zeros((HALO, d_lru), F32)
        xb_ext[0:HALO, :] = jnp.zeros((HALO, d_pool), F32)
        hstate[...] = jnp.zeros_like(hstate)

    a_cols = 2 * LANES
    n_hooks = d_lru // LANES + 2

    def stage_b(zbuf, znext):
        pending = list(range(0, win_ref.shape[1], a_cols))
        hooks_left = [n_hooks]

        def stage_a_chunks():
            n = -(-len(pending) // hooks_left[0])
            hooks_left[0] -= 1
            for _ in range(n):
                lo = pending.pop(0)
                znext[:, lo:lo + a_cols] = _dot(u_ref[...], win_ref[:, lo:lo + a_cols])

        conv_k = convw_ref.shape[0]
        xa_pre = zbuf[:, 0:d_lru]
        xa_ext[HALO:HALO + rows, :] = xa_pre
        xa = convb_ref[...] + convw_ref[conv_k - 1:conv_k, :] * xa_pre
        for j in range(1, conv_k):
            xa = xa + convw_ref[conv_k - 1 - j:conv_k - j, :] * xa_ext[HALO - j * SEQ_GROUP:HALO - j * SEQ_GROUP + rows, :]
        xa_ext[0:HALO, :] = xa_ext[rows:rows + HALO, :]

        for p in range(d_lru // LANES):
            stage_a_chunks()
            sl = slice(p * LANES, (p + 1) * LANES)
            xa_s = xa[:, sl]
            rg = _dot(xa_s.astype(BF16), wgate_ref[p])
            r = _sigmoid(rg[:, :LANES] + brg_ref[:, sl])
            i = _sigmoid(rg[:, LANES:] + big_ref[:, sl])
            log_a = r * (-LRU_C * _softplus(-lam_ref[:, sl]))
            a = jnp.exp(log_a)
            b = _sqrt_nonneg(1.0 - a * a) * (i * xa_s)
            h = hstate[:, sl]
            hs = []
            for t in range(steps):
                rs = slice(t * SEQ_GROUP, (t + 1) * SEQ_GROUP)
                h = a[rs] * h + b[rs]
                hs.append(h)
            hstate[:, sl] = h
            hseq = jnp.concatenate(hs, axis=0)
            ha_scr[:, sl] = (hseq * _gelu_tanh(zbuf[:, d_lru + p * LANES:d_lru + (p + 1) * LANES])).astype(BF16)

        stage_a_chunks()
        xb_ext[HALO:HALO + rows, :] = zbuf[:, o_pool:o_gate]
        t_glob = (c * steps + lax.broadcasted_iota(I32, (rows, LANES), 0) // SEQ_GROUP).astype(F32)
        for g in range(d_pool // LANES):
            sl = slice(g * LANES, (g + 1) * LANES)
            window = 2 << g
            acc = xb_ext[:, sl]
            span = 1
            while span < window:
                acc = acc[span * SEQ_GROUP:] + acc[:-span * SEQ_GROUP]
                span *= 2
            acc = acc[acc.shape[0] - rows:]
            xb_s = xb_ext[HALO:HALO + rows, sl]
            pooled = acc / jnp.minimum(t_glob + 1.0, float(window)) - xb_s
            mixed = _dot(pooled.astype(BF16), wpool_ref[g]) * pscale_ref[:, sl]
            hb_scr[:, sl] = mixed.astype(BF16)
        xb_ext[0:HALO, :] = xb_ext[rows:rows + HALO, :]

        stage_a_chunks()
        assert not pending
        merged = (_sigmoid(zbuf[:, o_gate:o_gate + d_model]) * _dot(ha_scr[...], wupa_ref[...])
                  + _sigmoid(zbuf[:, o_gate + d_model:o_gate + 2 * d_model]) * _dot(hb_scr[...], wupb_ref[...]))
        h_new = h_ref[...] + _dot(merged.astype(BF16), wout_ref[...])
        hmid_ref[...] = h_new

        u2 = _rms_norm(h_new, g2_ref[...])
        _store_row_slabs(u2_ref, _pack_bf16_pairs(u2))
        logits = _dot(u2.astype(BF16), wr_ref[...]) + br_ref[...]
        lane = lax.broadcasted_iota(I32, (rows, LANES), 1).astype(F32)
        neg = -jnp.inf
        big = float(LANES)
        is_group = lane < n_groups
        glog = jnp.where(is_group, logits, neg)
        gmax = jnp.max(glog, axis=-1, keepdims=True)
        gidx = jnp.min(jnp.where(glog == gmax, lane, big), axis=-1, keepdims=True)
        psum = jnp.sum(jnp.where(is_group, jnp.exp(glog - gmax), 0.0), axis=-1, keepdims=True)
        lo = n_groups + gidx * experts_per_group
        in_group = (lane >= lo) & (lane < lo + experts_per_group)
        sel = jnp.where(in_group, logits, neg)
        m1 = jnp.max(sel, axis=-1, keepdims=True)
        i1 = jnp.min(jnp.where(sel == m1, lane, big), axis=-1, keepdims=True)
        sel2 = jnp.where(lane == i1, neg, sel)
        m2 = jnp.max(sel2, axis=-1, keepdims=True)
        i2 = jnp.min(jnp.where(sel2 == m2, lane, big), axis=-1, keepdims=True)
        e21 = jnp.exp(m2 - m1)
        w1 = 1.0 / (psum * (1.0 + e21))
        w2 = e21 * w1
        e1 = i1 - n_groups
        e2 = i2 - n_groups
        route_ref[...] = jnp.where(lane == 0, e1, jnp.where(lane == 1, e2,
                                   jnp.where(lane == 2, w1, jnp.where(lane == 3, w2, 0.0))))
        chosen = jnp.where((lane == e1) | (lane == e2), 1.0, 0.0)
        cnt_scr[...] = cnt_scr[...] + jnp.sum(chosen, axis=0, keepdims=True)
        counts_ref[...] = cnt_scr[...]

    @pl.when(lax.rem(s, 2) == 0)
    def _():
        stage_b(z1, z0)

    @pl.when(lax.rem(s, 2) == 1)
    def _():
        stage_b(z0, z1)


def _prenorm_kernel(h_ref, g_ref, u_ref):
    u_ref[...] = _rms_norm(h_ref[...], g_ref[...]).astype(BF16)


def _rank_kernel(route_ref, counts_ref, pos_ref, tri_scr, run_scr, start_scr, *, tr, n_experts):
    i = pl.program_id(0)

    @pl.when(i == 0)
    def _():
        r_i = lax.broadcasted_iota(I32, (tr, tr), 0)
        c_i = lax.broadcasted_iota(I32, (tr, tr), 1)
        tri_scr[...] = jnp.where(r_i > c_i, 1.0, 0.0).astype(BF16)
        run_scr[...] = jnp.zeros_like(run_scr)
        cnt = counts_ref[...].astype(I32)
        padded = ((cnt + (EXPERT_BLOCK - 1)) // EXPERT_BLOCK) * EXPERT_BLOCK
        lane8 = lax.broadcasted_iota(I32, (SUBLANES, LANES), 1)
        incl = padded
        k = 1
        while k < n_experts:
            incl = incl + jnp.where(lane8 >= k, pltpu.roll(incl, k, 1), 0)
            k *= 2
        start_scr[...] = (incl - padded).astype(F32)

    r = route_ref[...]
    lane = lax.broadcasted_iota(I32, (tr, LANES), 1).astype(F32)
    oh0 = lane == r[:, 0:1]
    oh1 = lane == r[:, 1:2]
    oh = jnp.where(oh0 | oh1, 1.0, 0.0)
    before = _dot(tri_scr[...], oh.astype(BF16)) + (run_scr[0:1, :] + start_scr[0:1, :])
    rk0 = jnp.sum(jnp.where(oh0, before, 0.0), axis=-1, keepdims=True)
    rk1 = jnp.sum(jnp.where(oh1, before, 0.0), axis=-1, keepdims=True)
    cols = jnp.where(lane == 0, rk0, jnp.where(lane == 1, rk1, 0.0))
    pos_ref[...] = jnp.transpose(cols)[0:SUBLANES, :].astype(I32)
    run_scr[...] = run_scr[...] + jnp.sum(oh, axis=0, keepdims=True)


def _dispatch_kernel(cnt_ref, end_ref, pos_ref, u2_ref, xs_ref, zero_scr, sem, zsem, *, td, slabs, n_experts):
    @pl.when(pl.program_id(0) == 0)
    def _():
        zero_scr[...] = jnp.zeros_like(zero_scr)
        n_blocks = xs_ref.shape[0] // (EXPERT_BLOCK * slabs)
        n_used = end_ref[n_experts - 1] // EXPERT_BLOCK

        def zero_block(blk):
            row0 = pl.multiple_of(blk * (EXPERT_BLOCK * slabs), EXPERT_BLOCK * slabs)
            return pltpu.make_async_copy(zero_scr, xs_ref.at[pl.ds(row0, EXPERT_BLOCK * slabs)], zsem)

        todo = [(cnt_ref[e] > 0, end_ref[e] // EXPERT_BLOCK - 1) for e in range(n_experts)]
        todo += [(blk >= n_used, blk) for blk in range(max(n_blocks - n_experts, 0), n_blocks)]
        for cond, blk in todo:
            @pl.when(cond)
            def _():
                zero_block(blk).start()
        for cond, blk in todo:
            @pl.when(cond)
            def _():
                zero_block(blk).wait()

    def issue(q, carry):
        for uu in range(DMA_UNROLL):
            t = q * DMA_UNROLL + uu
            for k in range(TOP_K):
                dst = pl.multiple_of(pos_ref[k, t] * slabs, slabs)
                pltpu.make_async_copy(u2_ref.at[pl.ds(t * slabs, slabs)], xs_ref.at[pl.ds(dst, slabs)],
                                      sem).start(priority=k)
        return carry

    lax.fori_loop(0, td // DMA_UNROLL, issue, 0)
    for _ in range(TOP_K):
        pltpu.make_async_copy(u2_ref, xs_ref.at[pl.ds(0, td * slabs)], sem).wait()


def _expert_kernel(be_ref, nu_ref, xs_ref, wg_ref, wu_ref, wd_ref, ys_ref, wg_b, wu_b, wd_b, *, slabs):
    j = pl.program_id(0)

    @pl.when((j == 0) | (be_ref[j] != be_ref[jnp.maximum(j - 1, 0)]))
    def _():
        wg_b[...] = wg_ref[...].astype(BF16)
        wu_b[...] = wu_ref[...].astype(BF16)
        wd_b[...] = wd_ref[...].astype(BF16)

    @pl.when(j < nu_ref[0])
    def _():
        half = slabs * LANES
        lo, hi = _unpack_bf16_pairs(_load_row_slabs(xs_ref, 0, EXPERT_BLOCK, slabs))
        gate = _dot(lo, wg_b[0:half, :]) + _dot(hi, wg_b[half:, :])
        up = _dot(lo, wu_b[0:half, :]) + _dot(hi, wu_b[half:, :])
        act = (gate * _sigmoid(gate)) * up
        _store_row_slabs(ys_ref, _dot(act.astype(BF16), wd_b[...]))

    @pl.when(j >= nu_ref[0])
    def _():
        ys_ref[...] = jnp.zeros_like(ys_ref)


def _combine_kernel(pos_cur_ref, pos_nxt_ref, hmid_ref, route_ref, p_ref, g3_ref, wple_ref, wpg_ref, gn_ref,
                    ys_ref, out_ref, *rest, tp, slabs, final):
    if final:
        unext_ref = None
        ybuf0, ybuf1, sems = rest
    else:
        unext_ref, ybuf0, ybuf1, sems = rest
    i = pl.program_id(0)
    n = pl.num_programs(0)

    def gather_row(pos_ref, buf, s, t, k):
        src = pl.multiple_of(pos_ref[k, t] * slabs, slabs)
        return pltpu.make_async_copy(ys_ref.at[pl.ds(src, slabs)], buf.at[pl.ds((k * tp + t) * slabs, slabs)],
                                     sems.at[s])

    def wait_tile(buf, s):
        for k in range(TOP_K):
            pltpu.make_async_copy(ys_ref.at[pl.ds(0, tp * slabs)], buf.at[pl.ds(k * tp * slabs, tp * slabs)],
                                  sems.at[s]).wait()

    @pl.when(i == 0)
    def _():
        def body(q, carry):
            for uu in range(DMA_UNROLL):
                for k in range(TOP_K):
                    gather_row(pos_cur_ref, ybuf0, 0, q * DMA_UNROLL + uu, k).start(priority=k)
            return carry
        lax.fori_loop(0, tp // DMA_UNROLL, body, 0)

    def step(buf, s, nbuf, ns):
        wait_tile(buf, s)
        for t in range(tp):
            for k in range(TOP_K):
                gather_row(pos_nxt_ref, nbuf, ns, t, k).start(priority=k)
        r = route_ref[...]
        y0 = _load_row_slabs(buf, 0, tp, slabs)
        y1 = _load_row_slabs(buf, tp, tp, slabs)
        h2 = hmid_ref[...] + r[:, 2:3] * y0 + r[:, 3:4] * y1
        u3 = _rms_norm(h2, g3_ref[...]).astype(BF16)
        gate = _sigmoid(_dot(u3, wpg_ref[...]))
        h3 = h2 + gate * _dot(p_ref[...].astype(BF16), wple_ref[...])
        if final:
            out_ref[...] = _rms_norm(h3, gn_ref[...])
        else:
            out_ref[...] = h3
            unext_ref[...] = _rms_norm(h3, gn_ref[...]).astype(BF16)

        @pl.when(i == n - 1)
        def _():
            wait_tile(nbuf, ns)

    @pl.when(lax.rem(i, 2) == 0)
    def _():
        step(ybuf0, 0, ybuf1, 1)

    @pl.when(lax.rem(i, 2) == 1)
    def _():
        step(ybuf1, 1, ybuf0, 0)


def _const_spec(shape, layer):
    nd = len(shape)
    return pl.BlockSpec((None,) + tuple(shape), lambda i: (layer,) + (0,) * nd, pipeline_mode=pl.Buffered(1))


def _block_diag_gates(w_r, w_i):
    n_layers, n_heads, hd, _ = w_r.shape
    per = LANES // hd
    n_slab = n_heads // per

    def bd(w):
        w = w.reshape(n_layers, n_slab, per, hd, hd)
        eye = jnp.eye(per, dtype=w.dtype)
        out = jnp.einsum('lspij,pq->lspiqj', w, eye)
        return out.reshape(n_layers, n_slab, LANES, LANES)

    return jnp.concatenate([bd(w_r), bd(w_i)], axis=-1).astype(BF16)


def _tile_sizes(seq, n_tok):
    mixer_rows = min(256, seq * SEQ_GROUP)
    rank_rows = min(512, n_tok)
    combine_rows = min(256, n_tok)
    assert (seq * SEQ_GROUP) % mixer_rows == 0 and n_tok % rank_rows == 0 and rank_rows % combine_rows == 0
    assert mixer_rows % SEQ_GROUP == 0 and rank_rows % DMA_UNROLL == 0 and combine_rows % DMA_UNROLL == 0
    return mixer_rows, rank_rows, combine_rows


def kernel(x, p, norm_mix_g, w_in, conv_w, conv_b, w_rg, b_rg, w_ig, b_ig, lru_lambda, w_pool, pool_scale, w_up_a, w_up_b, w_out, norm_ffn_g, w_router_group, b_router_group, w_router_expert, b_router_expert, w_gate_e, w_up_e, w_down_e, norm_ple_g, w_ple, w_ple_gate, final_norm_g):
    bsz, seq, d_model = x.shape
    n_layers, _, d_in = w_in.shape
    d_lru = conv_w.shape[2]
    d_pool = pool_scale.shape[1]
    n_groups = w_router_group.shape[2]
    n_experts = w_router_expert.shape[2]
    experts_per_group = n_experts // n_groups
    d_expert = w_gate_e.shape[3]
    ple_dim = w_ple.shape[1]
    n_tok = bsz * seq
    half = d_model // 2
    x_slabs = half // LANES
    y_slabs = d_model // LANES

    rows_m, tr, tp = _tile_sizes(seq, n_tok)
    assert bsz % SEQ_GROUP == 0
    assert d_in == 2 * d_lru + d_pool + 2 * d_model
    assert LANES % w_rg.shape[2] == 0 and d_lru % LANES == 0 and d_model % (2 * LANES) == 0
    assert w_pool.shape[2] == LANES and (2 << (w_pool.shape[1] - 1)) <= HALO_STEPS and conv_w.shape[1] - 1 <= HALO_STEPS
    assert n_groups + n_experts <= LANES

    n_blocks = (n_tok * TOP_K + n_experts * (EXPERT_BLOCK - 1) + EXPERT_BLOCK - 1) // EXPERT_BLOCK
    n_rows = n_blocks * EXPERT_BLOCK

    w_in_b = w_in.astype(BF16)
    w_gate_bd = _block_diag_gates(w_rg, w_ig)
    w_pool_b = w_pool.astype(BF16)
    w_up_a_b = w_up_a.astype(BF16)
    w_up_b_b = w_up_b.astype(BF16)
    w_out_b = w_out.astype(BF16)
    w_router = jnp.concatenate([w_router_group, w_router_expert], axis=-1)
    w_router = jnp.pad(w_router, ((0, 0), (0, 0), (0, LANES - w_router.shape[-1]))).astype(BF16)
    b_router = jnp.concatenate([b_router_group, b_router_expert], axis=-1)
    b_router = jnp.pad(b_router, ((0, 0), (0, LANES - b_router.shape[-1])))[:, None, :]
    w_ple_b = w_ple.astype(BF16)
    w_pg_b = w_ple_gate.astype(BF16)
    row3 = lambda a: a[:, None, :]
    g1, g2, g3 = row3(norm_mix_g), row3(norm_ffn_g), row3(norm_ple_g)
    conv_b3, b_rg3, b_ig3, lam3, pscale3 = row3(conv_b), row3(b_rg), row3(b_ig), row3(lru_lambda), row3(pool_scale)
    gf = final_norm_g[None, :]

    n_sg = bsz // SEQ_GROUP
    h = x.reshape(n_sg, SEQ_GROUP, seq, d_model).transpose(0, 2, 1, 3).reshape(n_tok, d_model)
    p2 = p.reshape(n_layers, n_sg, SEQ_GROUP, seq, ple_dim).transpose(0, 1, 3, 2, 4).reshape(n_layers, n_tok, ple_dim)
    n_chunks = seq * SEQ_GROUP // rows_m
    n_mt = n_tok // rows_m
    n_rt = n_tok // tr
    per_rt = tr // tp
    n_ct = n_tok // tp

    u_norm = pl.pallas_call(
        _prenorm_kernel,
        grid=(n_rt,),
        in_specs=[pl.BlockSpec((tr, d_model), lambda i: (i, 0)), _const_spec((1, d_model), 0)],
        out_specs=pl.BlockSpec((tr, d_model), lambda i: (i, 0)),
        out_shape=jax.ShapeDtypeStruct((n_tok, d_model), BF16),
        compiler_params=pltpu.CompilerParams(dimension_semantics=("arbitrary",)),
        name="prenorm",
    )(h, g1)

    for layer in range(n_layers):
        cs = functools.partial(_const_spec, layer=layer)
        in_map = lambda s: (jnp.minimum(s, n_mt - 1), 0)
        out_map = lambda s: (jnp.maximum(s - 1, 0), 0)
        h_mid, u2, route, counts = pl.pallas_call(
            functools.partial(_mixer_kernel, rows=rows_m, n_chunks=n_chunks, d_model=d_model, d_lru=d_lru,
                              d_pool=d_pool, n_groups=n_groups, experts_per_group=experts_per_group),
            grid=(n_mt + 1,),
            in_specs=[
                pl.BlockSpec((rows_m, d_model), in_map),
                pl.BlockSpec((rows_m, d_model), out_map),
                cs((d_model, d_in)), cs(conv_w.shape[1:]), cs((1, d_lru)),
                cs(w_gate_bd.shape[1:]), cs((1, d_lru)), cs((1, d_lru)), cs((1, d_lru)),
                cs(w_pool.shape[1:]), cs((1, d_pool)), cs((d_lru, d_model)), cs((d_pool, d_model)),
                cs((d_model, d_model)), cs((1, d_model)), cs((d_model, LANES)), cs((1, LANES)),
            ],
            out_specs=[
                pl.BlockSpec((rows_m, d_model), out_map),
                pl.BlockSpec((rows_m * x_slabs, LANES), out_map),
                pl.BlockSpec((rows_m, LANES), out_map),
                pl.BlockSpec((SUBLANES, LANES), lambda s: (0, 0)),
            ],
            out_shape=[
                jax.ShapeDtypeStruct((n_tok, d_model), F32),
                jax.ShapeDtypeStruct((n_tok * x_slabs, LANES), U32),
                jax.ShapeDtypeStruct((n_tok, LANES), F32),
                jax.ShapeDtypeStruct((SUBLANES, LANES), F32),
            ],
            scratch_shapes=[
                pltpu.VMEM((rows_m, d_in), F32), pltpu.VMEM((rows_m, d_in), F32),
                pltpu.VMEM((HALO + rows_m, d_lru), F32),
                pltpu.VMEM((HALO + rows_m, d_pool), F32),
                pltpu.VMEM((SEQ_GROUP, d_lru), F32),
                pltpu.VMEM((rows_m, d_lru), BF16),
                pltpu.VMEM((rows_m, d_pool), BF16),
                pltpu.VMEM((SUBLANES, LANES), F32),
            ],
            compiler_params=pltpu.CompilerParams(dimension_semantics=("arbitrary",), vmem_limit_bytes=56 * MIB),
            name=f"mixer_l{layer}",
        )(u_norm, h, w_in_b, conv_w, conv_b3, w_gate_bd, b_rg3, b_ig3, lam3, w_pool_b, pscale3,
          w_up_a_b, w_up_b_b, w_out_b, g2, w_router, b_router)

        pos = pl.pallas_call(
            functools.partial(_rank_kernel, tr=tr, n_experts=n_experts),
            grid=(n_rt,),
            in_specs=[pl.BlockSpec((tr, LANES), lambda i: (i, 0)),
                      pl.BlockSpec((SUBLANES, LANES), lambda i: (0, 0))],
            out_specs=pl.BlockSpec((None, SUBLANES, tr), lambda i: (i, 0, 0)),
            out_shape=jax.ShapeDtypeStruct((n_rt, SUBLANES, tr), I32),
            scratch_shapes=[pltpu.VMEM((tr, tr), BF16), pltpu.VMEM((SUBLANES, LANES), F32),
                            pltpu.VMEM((SUBLANES, LANES), F32)],
            compiler_params=pltpu.CompilerParams(dimension_semantics=("arbitrary",)),
            name=f"rank_l{layer}",
        )(route, counts)

        cnt = counts[0, :n_experts].astype(I32)
        pad_end = jnp.cumsum(((cnt + EXPERT_BLOCK - 1) // EXPERT_BLOCK) * EXPERT_BLOCK)
        n_used = (pad_end[-1] // EXPERT_BLOCK).astype(I32)
        block_row0 = jnp.minimum(jnp.arange(n_blocks, dtype=I32), n_used - 1) * EXPERT_BLOCK
        block_expert = jnp.minimum(jnp.sum(pad_end[None, :] <= block_row0[:, None], axis=1), n_experts - 1).astype(I32)

        xs = pl.pallas_call(
            functools.partial(_dispatch_kernel, td=tr, slabs=x_slabs, n_experts=n_experts),
            grid_spec=pltpu.PrefetchScalarGridSpec(
                num_scalar_prefetch=2,
                grid=(n_rt,),
                in_specs=[
                    pl.BlockSpec((None, SUBLANES, tr), lambda i, cnt, end: (i, 0, 0), memory_space=pltpu.SMEM),
                    pl.BlockSpec((tr * x_slabs, LANES), lambda i, cnt, end: (i, 0)),
                ],
                out_specs=pl.BlockSpec(memory_space=pl.ANY),
                scratch_shapes=[pltpu.VMEM((EXPERT_BLOCK * x_slabs, LANES), U32),
                                pltpu.SemaphoreType.DMA, pltpu.SemaphoreType.DMA],
            ),
            out_shape=jax.ShapeDtypeStruct((n_rows * x_slabs, LANES), U32),
            compiler_params=pltpu.CompilerParams(dimension_semantics=("arbitrary",), has_side_effects=True),
            name=f"dispatch_l{layer}",
        )(cnt, pad_end, pos, u2)

        used_map = lambda j, be, nu: (jnp.minimum(j, nu[0] - 1), 0)
        w_map = lambda j, be, nu, layer=layer: (layer, be[j], 0, 0)
        ys = pl.pallas_call(
            functools.partial(_expert_kernel, slabs=x_slabs),
            grid_spec=pltpu.PrefetchScalarGridSpec(
                num_scalar_prefetch=2,
                grid=(n_blocks,),
                in_specs=[
                    pl.BlockSpec((EXPERT_BLOCK * x_slabs, LANES), used_map),
                    pl.BlockSpec((None, None, d_model, d_expert), w_map),
                    pl.BlockSpec((None, None, d_model, d_expert), w_map),
                    pl.BlockSpec((None, None, d_expert, d_model), w_map),
                ],
                out_specs=pl.BlockSpec((EXPERT_BLOCK * y_slabs, LANES), lambda j, be, nu: (j, 0)),
                scratch_shapes=[pltpu.VMEM((d_model, d_expert), BF16), pltpu.VMEM((d_model, d_expert), BF16),
                                pltpu.VMEM((d_expert, d_model), BF16)],
            ),
            out_shape=jax.ShapeDtypeStruct((n_rows * y_slabs, LANES), F32),
            compiler_params=pltpu.CompilerParams(dimension_semantics=("arbitrary",), vmem_limit_bytes=40 * MIB),
            name=f"experts_l{layer}",
        )(block_expert, n_used[None], xs, w_gate_e, w_up_e, w_down_e)

        final = layer == n_layers - 1
        pos_spec = lambda shift: pl.BlockSpec(
            (None, SUBLANES, tp),
            lambda i: (jnp.minimum(i + shift, n_ct - 1) // per_rt, 0, jnp.minimum(i + shift, n_ct - 1) % per_rt),
            memory_space=pltpu.SMEM)
        tok_spec = pl.BlockSpec((tp, d_model), lambda i: (i, 0))
        res = pl.pallas_call(
            functools.partial(_combine_kernel, tp=tp, slabs=y_slabs, final=final),
            grid=(n_ct,),
            in_specs=[
                pos_spec(0), pos_spec(1),
                tok_spec,
                pl.BlockSpec((tp, LANES), lambda i: (i, 0)),
                pl.BlockSpec((None, tp, ple_dim), lambda i, layer=layer: (layer, i, 0)),
                cs((1, d_model)), cs((ple_dim, d_model)), cs((d_model, d_model)),
                pl.BlockSpec((1, d_model), lambda i: (0, 0)) if final else _const_spec((1, d_model), layer + 1),
                pl.BlockSpec(memory_space=pl.ANY),
            ],
            out_specs=tok_spec if final else [tok_spec, tok_spec],
            out_shape=(jax.ShapeDtypeStruct((n_tok, d_model), F32) if final else
                       [jax.ShapeDtypeStruct((n_tok, d_model), F32), jax.ShapeDtypeStruct((n_tok, d_model), BF16)]),
            scratch_shapes=[pltpu.VMEM((TOP_K * tp * y_slabs, LANES), F32),
                            pltpu.VMEM((TOP_K * tp * y_slabs, LANES), F32),
                            pltpu.SemaphoreType.DMA((2,))],
            compiler_params=pltpu.CompilerParams(dimension_semantics=("arbitrary",)),
            name=f"combine_l{layer}",
        )(pos, pos, h_mid, route, p2, g3, w_ple_b, w_pg_b, gf if final else g1, ys)
        h, u_norm = (res, None) if final else res

    out = h.reshape(n_sg, seq, SEQ_GROUP, d_model).transpose(0, 2, 1, 3)
    return out.reshape(bsz, seq, d_model)
```

```python
import functools

import jax
import jax.numpy as jnp
from jax import lax
from jax.experimental import pallas as pl
from jax.experimental.pallas import tpu as pltpu

EPS = 1e-6
LRU_C = 8.0
EXPERT_BLOCK = 256
EXPERT_PAIR = 2
TOP_K = 2
GELU_C0 = 0.7978845608028654
GELU_C1 = 0.044715

LANES = 128
SUBLANES = 8
SEQ_GROUP = SUBLANES
HALO_STEPS = 16
HALO = HALO_STEPS * SEQ_GROUP
DMA_UNROLL = 8

F32 = jnp.float32
BF16 = jnp.bfloat16
U32 = jnp.uint32
I32 = jnp.int32
MIB = 1024 * 1024


def _dot(a, b):
    return jnp.dot(a, b, preferred_element_type=F32)


def _sigmoid(x):
    return 1.0 / (1.0 + jnp.exp(-x))


def _rms_norm(x, g):
    return x * lax.rsqrt(jnp.mean(x * x, axis=-1, keepdims=True) + EPS) * g


def _gelu_tanh(x):
    return 0.5 * x * (1.0 + jnp.tanh(GELU_C0 * (x + GELU_C1 * (x * x * x))))


def _softplus(x):
    return jnp.maximum(x, 0.0) + jnp.log1p(jnp.exp(-jnp.abs(x)))


def _sqrt_nonneg(x):
    return jnp.where(x > 0.0, x * lax.rsqrt(x), 0.0)


def _pack_bf16_pairs(x):
    half = x.shape[1] // 2
    bits = lax.bitcast_convert_type(x.astype(BF16).astype(F32), U32)
    return (bits[:, :half] >> 16) | (bits[:, half:] & jnp.uint32(0xFFFF0000))


def _unpack_bf16_pairs(w):
    lo = lax.bitcast_convert_type(w << 16, F32).astype(BF16)
    hi = lax.bitcast_convert_type(w & jnp.uint32(0xFFFF0000), F32).astype(BF16)
    return lo, hi


def _unpack_pairs_f32(w):
    return jnp.concatenate([lax.bitcast_convert_type(w << 16, F32),
                            lax.bitcast_convert_type(w & jnp.uint32(0xFFFF0000), F32)], axis=1)


def _store_row_slabs(ref, val, row0=0):
    n = val.shape[1] // LANES
    for c in range(n):
        ref[pl.ds(row0 * n + c, val.shape[0], stride=n), :] = val[:, c * LANES:(c + 1) * LANES]


def _load_row_slabs(ref, row0, n_rows, n):
    return jnp.concatenate([ref[pl.ds(row0 * n + c, n_rows, stride=n), :] for c in range(n)], axis=1)


def _mixer_kernel(u_ref, h_ref, win_ref, convw_ref, convb_ref, wgate_ref, brg_ref, big_ref, lam_ref,
                  wpool_ref, pscale_ref, wupa_ref, wupb_ref, wout_ref, g2_ref, wr_ref, br_ref,
                  hmid_ref, u2_ref, route_ref, counts_ref,
                  z0, z1, xa_ext, xb_ext, hstate, ha_scr, hb_scr, cnt_scr,
                  *, rows, n_chunks, d_model, d_lru, d_pool, n_groups, experts_per_group):
    s = pl.program_id(0)
    steps = rows // SEQ_GROUP
    o_pool = 2 * d_lru
    o_gate = o_pool + d_pool
    c = lax.rem(jnp.maximum(s - 1, 0), n_chunks)

    @pl.when(s == 0)
    def _():
        z1[...] = jnp.zeros_like(z1)

    @pl.when(s <= 1)
    def _():
        cnt_scr[...] = jnp.zeros_like(cnt_scr)

    @pl.when(c == 0)
    def _():
        xa_ext[0:HALO, :] = jnp.zeros((HALO, d_lru), F32)
        xb_ext[0:HALO, :] = jnp.zeros((HALO, d_pool), F32)
        hstate[...] = jnp.zeros_like(hstate)

    a_cols = 2 * LANES
    n_hooks = d_lru // LANES + 2

    def stage_b(zbuf, znext):
        pending = list(range(0, win_ref.shape[1], a_cols))
        hooks_left = [n_hooks]

        def stage_a_chunks():
            n = -(-len(pending) // hooks_left[0])
            hooks_left[0] -= 1
            for _ in range(n):
                lo = pending.pop(0)
                znext[:, lo:lo + a_cols] = _dot(u_ref[...], win_ref[:, lo:lo + a_cols])

        conv_k = convw_ref.shape[0]
        xa_pre = zbuf[:, 0:d_lru]
        xa_ext[HALO:HALO + rows, :] = xa_pre
        xa = convb_ref[...] + convw_ref[conv_k - 1:conv_k, :] * xa_pre
        for j in range(1, conv_k):
            xa = xa + convw_ref[conv_k - 1 - j:conv_k - j, :] * xa_ext[HALO - j * SEQ_GROUP:HALO - j * SEQ_GROUP + rows, :]
        xa_ext[0:HALO, :] = xa_ext[rows:rows + HALO, :]

        for p in range(d_lru // LANES):
            stage_a_chunks()
            sl = slice(p * LANES, (p + 1) * LANES)
            xa_s = xa[:, sl]
            rg = _dot(xa_s.astype(BF16), wgate_ref[p])
            r = _sigmoid(rg[:, :LANES] + brg_ref[:, sl])
            i = _sigmoid(rg[:, LANES:] + big_ref[:, sl])
            log_a = r * (-LRU_C * _softplus(-lam_ref[:, sl]))
            a = jnp.exp(log_a)
            b = _sqrt_nonneg(1.0 - a * a) * (i * xa_s)
            h = hstate[:, sl]
            hs = []
            for t in range(steps):
                rs = slice(t * SEQ_GROUP, (t + 1) * SEQ_GROUP)
                h = a[rs] * h + b[rs]
                hs.append(h)
            hstate[:, sl] = h
            hseq = jnp.concatenate(hs, axis=0)
            ha_scr[:, sl] = (hseq * _gelu_tanh(zbuf[:, d_lru + p * LANES:d_lru + (p + 1) * LANES])).astype(BF16)

        stage_a_chunks()
        xb_ext[HALO:HALO + rows, :] = zbuf[:, o_pool:o_gate]
        t_glob = (c * steps + lax.broadcasted_iota(I32, (rows, LANES), 0) // SEQ_GROUP).astype(F32)
        for g in range(d_pool // LANES):
            sl = slice(g * LANES, (g + 1) * LANES)
            window = 2 << g
            acc = xb_ext[:, sl]
            span = 1
            while span < window:
                acc = acc[span * SEQ_GROUP:] + acc[:-span * SEQ_GROUP]
                span *= 2
            acc = acc[acc.shape[0] - rows:]
            xb_s = xb_ext[HALO:HALO + rows, sl]
            pooled = acc / jnp.minimum(t_glob + 1.0, float(window)) - xb_s
            mixed = _dot(pooled.astype(BF16), wpool_ref[g]) * pscale_ref[:, sl]
            hb_scr[:, sl] = mixed.astype(BF16)
        xb_ext[0:HALO, :] = xb_ext[rows:rows + HALO, :]

        stage_a_chunks()
        assert not pending
        merged = (_sigmoid(zbuf[:, o_gate:o_gate + d_model]) * _dot(ha_scr[...], wupa_ref[...])
                  + _sigmoid(zbuf[:, o_gate + d_model:o_gate + 2 * d_model]) * _dot(hb_scr[...], wupb_ref[...]))
        h_new = h_ref[...] + _dot(merged.astype(BF16), wout_ref[...])
        hmid_ref[...] = h_new

        u2 = _rms_norm(h_new, g2_ref[...])
        _store_row_slabs(u2_ref, _pack_bf16_pairs(u2))
        logits = _dot(u2.astype(BF16), wr_ref[...]) + br_ref[...]
        lane = lax.broadcasted_iota(I32, (rows, LANES), 1).astype(F32)
        neg = -jnp.inf
        big = float(LANES)
        is_group = lane < n_groups
        glog = jnp.where(is_group, logits, neg)
        gmax = jnp.max(glog, axis=-1, keepdims=True)
        gidx = jnp.min(jnp.where(glog == gmax, lane, big), axis=-1, keepdims=True)
        psum = jnp.sum(jnp.where(is_group, jnp.exp(glog - gmax), 0.0), axis=-1, keepdims=True)
        lo = n_groups + gidx * experts_per_group
        in_group = (lane >= lo) & (lane < lo + experts_per_group)
        sel = jnp.where(in_group, logits, neg)
        m1 = jnp.max(sel, axis=-1, keepdims=True)
        i1 = jnp.min(jnp.where(sel == m1, lane, big), axis=-1, keepdims=True)
        sel2 = jnp.where(lane == i1, neg, sel)
        m2 = jnp.max(sel2, axis=-1, keepdims=True)
        i2 = jnp.min(jnp.where(sel2 == m2, lane, big), axis=-1, keepdims=True)
        e21 = jnp.exp(m2 - m1)
        w1 = 1.0 / (psum * (1.0 + e21))
        w2 = e21 * w1
        e1 = i1 - n_groups
        e2 = i2 - n_groups
        route_ref[...] = jnp.where(lane == 0, e1, jnp.where(lane == 1, e2,
                                   jnp.where(lane == 2, w1, jnp.where(lane == 3, w2, 0.0))))
        chosen = jnp.where((lane == e1) | (lane == e2), 1.0, 0.0)
        cnt_scr[...] = cnt_scr[...] + jnp.sum(chosen, axis=0, keepdims=True)
        counts_ref[...] = cnt_scr[...]

    @pl.when(lax.rem(s, 2) == 0)
    def _():
        stage_b(z1, z0)

    @pl.when(lax.rem(s, 2) == 1)
    def _():
        stage_b(z0, z1)


def _prenorm_kernel(h_ref, g_ref, u_ref):
    u_ref[...] = _rms_norm(h_ref[...], g_ref[...]).astype(BF16)


def _rank_kernel(route_ref, counts_ref, pos_ref, tri_scr, run_scr, start_scr, *, tr, n_experts):
    i = pl.program_id(0)

    @pl.when(i == 0)
    def _():
        r_i = lax.broadcasted_iota(I32, (tr, tr), 0)
        c_i = lax.broadcasted_iota(I32, (tr, tr), 1)
        tri_scr[...] = jnp.where(r_i > c_i, 1.0, 0.0).astype(BF16)
        run_scr[...] = jnp.zeros_like(run_scr)
        cnt = counts_ref[...].astype(I32)
        padded = ((cnt + (EXPERT_BLOCK - 1)) // EXPERT_BLOCK) * EXPERT_BLOCK
        lane8 = lax.broadcasted_iota(I32, (SUBLANES, LANES), 1)
        incl = padded
        k = 1
        while k < n_experts:
            incl = incl + jnp.where(lane8 >= k, pltpu.roll(incl, k, 1), 0)
            k *= 2
        start_scr[...] = (incl - padded).astype(F32)

    r = route_ref[...]
    lane = lax.broadcasted_iota(I32, (tr, LANES), 1).astype(F32)
    oh0 = lane == r[:, 0:1]
    oh1 = lane == r[:, 1:2]
    oh = jnp.where(oh0 | oh1, 1.0, 0.0)
    before = _dot(tri_scr[...], oh.astype(BF16)) + (run_scr[0:1, :] + start_scr[0:1, :])
    rk0 = jnp.sum(jnp.where(oh0, before, 0.0), axis=-1, keepdims=True)
    rk1 = jnp.sum(jnp.where(oh1, before, 0.0), axis=-1, keepdims=True)
    cols = jnp.where(lane == 0, rk0, jnp.where(lane == 1, rk1, 0.0))
    pos_ref[...] = jnp.transpose(cols)[0:SUBLANES, :].astype(I32)
    run_scr[...] = run_scr[...] + jnp.sum(oh, axis=0, keepdims=True)


def _dispatch_kernel(cnt_ref, end_ref, pos_ref, u2_ref, xs_ref, zero_scr, sem, zsem, *, td, slabs, n_experts):
    @pl.when(pl.program_id(0) == 0)
    def _():
        zero_scr[...] = jnp.zeros_like(zero_scr)
        n_blocks = xs_ref.shape[0] // (EXPERT_BLOCK * slabs)
        n_used = end_ref[n_experts - 1] // EXPERT_BLOCK

        def zero_block(blk):
            row0 = pl.multiple_of(blk * (EXPERT_BLOCK * slabs), EXPERT_BLOCK * slabs)
            return pltpu.make_async_copy(zero_scr, xs_ref.at[pl.ds(row0, EXPERT_BLOCK * slabs)], zsem)

        todo = [(cnt_ref[e] > 0, end_ref[e] // EXPERT_BLOCK - 1) for e in range(n_experts)]
        todo += [(blk >= n_used, blk) for blk in range(max(n_blocks - n_experts - EXPERT_PAIR + 1, 0), n_blocks)]
        for cond, blk in todo:
            @pl.when(cond)
            def _():
                zero_block(blk).start()
        for cond, blk in todo:
            @pl.when(cond)
            def _():
                zero_block(blk).wait()

    def issue(q, carry):
        for uu in range(DMA_UNROLL):
            t = q * DMA_UNROLL + uu
            for k in range(TOP_K):
                dst = pl.multiple_of(pos_ref[k, t] * slabs, slabs)
                pltpu.make_async_copy(u2_ref.at[pl.ds(t * slabs, slabs)], xs_ref.at[pl.ds(dst, slabs)],
                                      sem).start(priority=k)
        return carry

    lax.fori_loop(0, td // DMA_UNROLL, issue, 0)
    for _ in range(TOP_K):
        pltpu.make_async_copy(u2_ref, xs_ref.at[pl.ds(0, td * slabs)], sem).wait()


def _expert_kernel(be_ref, nu_ref, xs_ref, *rest, slabs):
    w_f32 = (rest[0:3], rest[3:6])
    ys_ref = rest[6]
    w_bf16 = (rest[7:10], rest[10:13])
    i = pl.program_id(0)
    half = slabs * LANES
    blk_rows = EXPERT_BLOCK * slabs

    for b in range(EXPERT_PAIR):
        j = EXPERT_PAIR * i + b

        @pl.when((i == 0) | (be_ref[j] != be_ref[jnp.maximum(j - EXPERT_PAIR, 0)]))
        def _():
            for dst, src in zip(w_bf16[b], w_f32[b]):
                dst[...] = src[...].astype(BF16)

    def ffn(b):
        wg_b, wu_b, wd_b = w_bf16[b]
        lo, hi = _unpack_bf16_pairs(_load_row_slabs(xs_ref, b * EXPERT_BLOCK, EXPERT_BLOCK, slabs))
        gate = _dot(lo, wg_b[0:half, :]) + _dot(hi, wg_b[half:, :])
        up = _dot(lo, wu_b[0:half, :]) + _dot(hi, wu_b[half:, :])
        act = (gate * _sigmoid(gate)) * up
        _store_row_slabs(ys_ref, _pack_bf16_pairs(_dot(act.astype(BF16), wd_b[...])), b * EXPERT_BLOCK)

    def zero(b):
        ys_ref[pl.ds(b * blk_rows, blk_rows), :] = jnp.zeros((blk_rows, LANES), U32)

    n_valid = nu_ref[0] - EXPERT_PAIR * i

    @pl.when(n_valid >= 2)
    def _():
        ffn(0)
        ffn(1)

    @pl.when(n_valid == 1)
    def _():
        ffn(0)
        zero(1)

    @pl.when(n_valid <= 0)
    def _():
        zero(0)
        zero(1)


def _combine_kernel(pos_cur_ref, pos_nxt_ref, hmid_ref, route_ref, p_ref, g3_ref, wple_ref, wpg_ref, gn_ref,
                    ys_ref, out_ref, *rest, tp, slabs, final):
    if final:
        unext_ref = None
        ybuf0, ybuf1, sems = rest
    else:
        unext_ref, ybuf0, ybuf1, sems = rest
    i = pl.program_id(0)
    n = pl.num_programs(0)

    def gather_row(pos_ref, buf, s, t, k):
        src = pl.multiple_of(pos_ref[k, t] * slabs, slabs)
        return pltpu.make_async_copy(ys_ref.at[pl.ds(src, slabs)], buf.at[pl.ds((k * tp + t) * slabs, slabs)],
                                     sems.at[s])

    def wait_tile(buf, s):
        for k in range(TOP_K):
            pltpu.make_async_copy(ys_ref.at[pl.ds(0, tp * slabs)], buf.at[pl.ds(k * tp * slabs, tp * slabs)],
                                  sems.at[s]).wait()

    @pl.when(i == 0)
    def _():
        def body(q, carry):
            for uu in range(DMA_UNROLL):
                for k in range(TOP_K):
                    gather_row(pos_cur_ref, ybuf0, 0, q * DMA_UNROLL + uu, k).start(priority=k)
            return carry
        lax.fori_loop(0, tp // DMA_UNROLL, body, 0)

    def step(buf, s, nbuf, ns):
        wait_tile(buf, s)
        for t in range(tp):
            for k in range(TOP_K):
                gather_row(pos_nxt_ref, nbuf, ns, t, k).start(priority=k)
        r = route_ref[...]
        y0 = _unpack_pairs_f32(_load_row_slabs(buf, 0, tp, slabs))
        y1 = _unpack_pairs_f32(_load_row_slabs(buf, tp, tp, slabs))
        h2 = hmid_ref[...] + r[:, 2:3] * y0 + r[:, 3:4] * y1
        u3 = _rms_norm(h2, g3_ref[...]).astype(BF16)
        gate = _sigmoid(_dot(u3, wpg_ref[...]))
        h3 = h2 + gate * _dot(p_ref[...].astype(BF16), wple_ref[...])
        if final:
            out_ref[...] = _rms_norm(h3, gn_ref[...])
        else:
            out_ref[...] = h3
            unext_ref[...] = _rms_norm(h3, gn_ref[...]).astype(BF16)

        @pl.when(i == n - 1)
        def _():
            wait_tile(nbuf, ns)

    @pl.when(lax.rem(i, 2) == 0)
    def _():
        step(ybuf0, 0, ybuf1, 1)

    @pl.when(lax.rem(i, 2) == 1)
    def _():
        step(ybuf1, 1, ybuf0, 0)


def _const_spec(shape, layer):
    nd = len(shape)
    return pl.BlockSpec((None,) + tuple(shape), lambda i: (layer,) + (0,) * nd, pipeline_mode=pl.Buffered(1))


def _block_diag_gates(w_r, w_i):
    n_layers, n_heads, hd, _ = w_r.shape
    per = LANES // hd
    n_slab = n_heads // per

    def bd(w):
        w = w.reshape(n_layers, n_slab, per, hd, hd)
        eye = jnp.eye(per, dtype=w.dtype)
        out = jnp.einsum('lspij,pq->lspiqj', w, eye)
        return out.reshape(n_layers, n_slab, LANES, LANES)

    return jnp.concatenate([bd(w_r), bd(w_i)], axis=-1).astype(BF16)


def _tile_sizes(seq, n_tok):
    mixer_rows = min(256, seq * SEQ_GROUP)
    rank_rows = min(512, n_tok)
    combine_rows = min(256, n_tok)
    assert (seq * SEQ_GROUP) % mixer_rows == 0 and n_tok % rank_rows == 0 and rank_rows % combine_rows == 0
    assert mixer_rows % SEQ_GROUP == 0 and rank_rows % DMA_UNROLL == 0 and combine_rows % DMA_UNROLL == 0
    return mixer_rows, rank_rows, combine_rows


def kernel(x, p, norm_mix_g, w_in, conv_w, conv_b, w_rg, b_rg, w_ig, b_ig, lru_lambda, w_pool, pool_scale, w_up_a, w_up_b, w_out, norm_ffn_g, w_router_group, b_router_group, w_router_expert, b_router_expert, w_gate_e, w_up_e, w_down_e, norm_ple_g, w_ple, w_ple_gate, final_norm_g):
    bsz, seq, d_model = x.shape
    n_layers, _, d_in = w_in.shape
    d_lru = conv_w.shape[2]
    d_pool = pool_scale.shape[1]
    n_groups = w_router_group.shape[2]
    n_experts = w_router_expert.shape[2]
    experts_per_group = n_experts // n_groups
    d_expert = w_gate_e.shape[3]
    ple_dim = w_ple.shape[1]
    n_tok = bsz * seq
    half = d_model // 2
    x_slabs = half // LANES

    rows_m, tr, tp = _tile_sizes(seq, n_tok)
    assert bsz % SEQ_GROUP == 0
    assert d_in == 2 * d_lru + d_pool + 2 * d_model
    assert LANES % w_rg.shape[2] == 0 and d_lru % LANES == 0 and d_model % (2 * LANES) == 0
    assert w_pool.shape[2] == LANES and (2 << (w_pool.shape[1] - 1)) <= HALO_STEPS and conv_w.shape[1] - 1 <= HALO_STEPS
    assert n_groups + n_experts <= LANES

    n_blocks = (n_tok * TOP_K + n_experts * (EXPERT_BLOCK - 1) + EXPERT_BLOCK - 1) // EXPERT_BLOCK
    n_blocks = -(-n_blocks // EXPERT_PAIR) * EXPERT_PAIR
    n_rows = n_blocks * EXPERT_BLOCK

    w_in_b = w_in.astype(BF16)
    w_gate_bd = _block_diag_gates(w_rg, w_ig)
    w_pool_b = w_pool.astype(BF16)
    w_up_a_b = w_up_a.astype(BF16)
    w_up_b_b = w_up_b.astype(BF16)
    w_out_b = w_out.astype(BF16)
    w_router = jnp.concatenate([w_router_group, w_router_expert], axis=-1)
    w_router = jnp.pad(w_router, ((0, 0), (0, 0), (0, LANES - w_router.shape[-1]))).astype(BF16)
    b_router = jnp.concatenate([b_router_group, b_router_expert], axis=-1)
    b_router = jnp.pad(b_router, ((0, 0), (0, LANES - b_router.shape[-1])))[:, None, :]
    w_ple_b = w_ple.astype(BF16)
    w_pg_b = w_ple_gate.astype(BF16)
    row3 = lambda a: a[:, None, :]
    g1, g2, g3 = row3(norm_mix_g), row3(norm_ffn_g), row3(norm_ple_g)
    conv_b3, b_rg3, b_ig3, lam3, pscale3 = row3(conv_b), row3(b_rg), row3(b_ig), row3(lru_lambda), row3(pool_scale)
    gf = final_norm_g[None, :]

    n_sg = bsz // SEQ_GROUP
    h = x.reshape(n_sg, SEQ_GROUP, seq, d_model).transpose(0, 2, 1, 3).reshape(n_tok, d_model)
    p2 = p.reshape(n_layers, n_sg, SEQ_GROUP, seq, ple_dim).transpose(0, 1, 3, 2, 4).reshape(n_layers, n_tok, ple_dim)
    n_chunks = seq * SEQ_GROUP // rows_m
    n_mt = n_tok // rows_m
    n_rt = n_tok // tr
    per_rt = tr // tp
    n_ct = n_tok // tp

    t_pre = min(2048, n_tok)
    assert n_tok % t_pre == 0
    u_norm = pl.pallas_call(
        _prenorm_kernel,
        grid=(n_tok // t_pre,),
        in_specs=[pl.BlockSpec((t_pre, d_model), lambda i: (i, 0)), _const_spec((1, d_model), 0)],
        out_specs=pl.BlockSpec((t_pre, d_model), lambda i: (i, 0)),
        out_shape=jax.ShapeDtypeStruct((n_tok, d_model), BF16),
        compiler_params=pltpu.CompilerParams(dimension_semantics=("arbitrary",)),
        name="prenorm",
    )(h, g1)

    for layer in range(n_layers):
        cs = functools.partial(_const_spec, layer=layer)
        in_map = lambda s: (jnp.minimum(s, n_mt - 1), 0)
        out_map = lambda s: (jnp.maximum(s - 1, 0), 0)
        h_mid, u2, route, counts = pl.pallas_call(
            functools.partial(_mixer_kernel, rows=rows_m, n_chunks=n_chunks, d_model=d_model, d_lru=d_lru,
                              d_pool=d_pool, n_groups=n_groups, experts_per_group=experts_per_group),
            grid=(n_mt + 1,),
            in_specs=[
                pl.BlockSpec((rows_m, d_model), in_map),
                pl.BlockSpec((rows_m, d_model), out_map),
                cs((d_model, d_in)), cs(conv_w.shape[1:]), cs((1, d_lru)),
                cs(w_gate_bd.shape[1:]), cs((1, d_lru)), cs((1, d_lru)), cs((1, d_lru)),
                cs(w_pool.shape[1:]), cs((1, d_pool)), cs((d_lru, d_model)), cs((d_pool, d_model)),
                cs((d_model, d_model)), cs((1, d_model)), cs((d_model, LANES)), cs((1, LANES)),
            ],
            out_specs=[
                pl.BlockSpec((rows_m, d_model), out_map),
                pl.BlockSpec((rows_m * x_slabs, LANES), out_map),
                pl.BlockSpec((rows_m, LANES), out_map),
                pl.BlockSpec((SUBLANES, LANES), lambda s: (0, 0)),
            ],
            out_shape=[
                jax.ShapeDtypeStruct((n_tok, d_model), F32),
                jax.ShapeDtypeStruct((n_tok * x_slabs, LANES), U32),
                jax.ShapeDtypeStruct((n_tok, LANES), F32),
                jax.ShapeDtypeStruct((SUBLANES, LANES), F32),
            ],
            scratch_shapes=[
                pltpu.VMEM((rows_m, d_in), F32), pltpu.VMEM((rows_m, d_in), F32),
                pltpu.VMEM((HALO + rows_m, d_lru), F32),
                pltpu.VMEM((HALO + rows_m, d_pool), F32),
                pltpu.VMEM((SEQ_GROUP, d_lru), F32),
                pltpu.VMEM((rows_m, d_lru), BF16),
                pltpu.VMEM((rows_m, d_pool), BF16),
                pltpu.VMEM((SUBLANES, LANES), F32),
            ],
            compiler_params=pltpu.CompilerParams(dimension_semantics=("arbitrary",), vmem_limit_bytes=56 * MIB),
            name=f"mixer_l{layer}",
        )(u_norm, h, w_in_b, conv_w, conv_b3, w_gate_bd, b_rg3, b_ig3, lam3, w_pool_b, pscale3,
          w_up_a_b, w_up_b_b, w_out_b, g2, w_router, b_router)

        pos = pl.pallas_call(
            functools.partial(_rank_kernel, tr=tr, n_experts=n_experts),
            grid=(n_rt,),
            in_specs=[pl.BlockSpec((tr, LANES), lambda i: (i, 0)),
                      pl.BlockSpec((SUBLANES, LANES), lambda i: (0, 0))],
            out_specs=pl.BlockSpec((None, SUBLANES, tr), lambda i: (i, 0, 0)),
            out_shape=jax.ShapeDtypeStruct((n_rt, SUBLANES, tr), I32),
            scratch_shapes=[pltpu.VMEM((tr, tr), BF16), pltpu.VMEM((SUBLANES, LANES), F32),
                            pltpu.VMEM((SUBLANES, LANES), F32)],
            compiler_params=pltpu.CompilerParams(dimension_semantics=("arbitrary",)),
            name=f"rank_l{layer}",
        )(route, counts)

        cnt = counts[0, :n_experts].astype(I32)
        pad_end = jnp.cumsum(((cnt + EXPERT_BLOCK - 1) // EXPERT_BLOCK) * EXPERT_BLOCK)
        n_used = (pad_end[-1] // EXPERT_BLOCK).astype(I32)
        block_row0 = jnp.minimum(jnp.arange(n_blocks, dtype=I32), n_used - 1) * EXPERT_BLOCK
        block_expert = jnp.minimum(jnp.sum(pad_end[None, :] <= block_row0[:, None], axis=1), n_experts - 1).astype(I32)

        xs = pl.pallas_call(
            functools.partial(_dispatch_kernel, td=tr, slabs=x_slabs, n_experts=n_experts),
            grid_spec=pltpu.PrefetchScalarGridSpec(
                num_scalar_prefetch=2,
                grid=(n_rt,),
                in_specs=[
                    pl.BlockSpec((None, SUBLANES, tr), lambda i, cnt, end: (i, 0, 0), memory_space=pltpu.SMEM),
                    pl.BlockSpec((tr * x_slabs, LANES), lambda i, cnt, end: (i, 0)),
                ],
                out_specs=pl.BlockSpec(memory_space=pl.ANY),
                scratch_shapes=[pltpu.VMEM((EXPERT_BLOCK * x_slabs, LANES), U32),
                                pltpu.SemaphoreType.DMA, pltpu.SemaphoreType.DMA],
            ),
            out_shape=jax.ShapeDtypeStruct((n_rows * x_slabs, LANES), U32),
            compiler_params=pltpu.CompilerParams(dimension_semantics=("arbitrary",), has_side_effects=True),
            name=f"dispatch_l{layer}",
        )(cnt, pad_end, pos, u2)

        pair_rows = EXPERT_PAIR * EXPERT_BLOCK * x_slabs
        used_map = lambda i, be, nu: (jnp.minimum(i, (nu[0] - 1) // EXPERT_PAIR), 0)
        w_specs, w_scratch = [], []
        for b in range(EXPERT_PAIR):
            w_map = lambda i, be, nu, layer=layer, b=b: (layer, be[EXPERT_PAIR * i + b], 0, 0)
            for shape in ((d_model, d_expert), (d_model, d_expert), (d_expert, d_model)):
                w_specs.append(pl.BlockSpec((None, None) + shape, w_map))
                w_scratch.append(pltpu.VMEM(shape, BF16))
        ys = pl.pallas_call(
            functools.partial(_expert_kernel, slabs=x_slabs),
            grid_spec=pltpu.PrefetchScalarGridSpec(
                num_scalar_prefetch=2,
                grid=(n_blocks // EXPERT_PAIR,),
                in_specs=[pl.BlockSpec((pair_rows, LANES), used_map)] + w_specs,
                out_specs=pl.BlockSpec((pair_rows, LANES), lambda i, be, nu: (i, 0)),
                scratch_shapes=w_scratch,
            ),
            out_shape=jax.ShapeDtypeStruct((n_rows * x_slabs, LANES), U32),
            compiler_params=pltpu.CompilerParams(dimension_semantics=("arbitrary",), vmem_limit_bytes=48 * MIB),
            name=f"experts_l{layer}",
        )(block_expert, n_used[None], xs, *([w_gate_e, w_up_e, w_down_e] * EXPERT_PAIR))

        final = layer == n_layers - 1
        pos_spec = lambda shift: pl.BlockSpec(
            (None, SUBLANES, tp),
            lambda i: (jnp.minimum(i + shift, n_ct - 1) // per_rt, 0, jnp.minimum(i + shift, n_ct - 1) % per_rt),
            memory_space=pltpu.SMEM)
        tok_spec = pl.BlockSpec((tp, d_model), lambda i: (i, 0))
        res = pl.pallas_call(
            functools.partial(_combine_kernel, tp=tp, slabs=x_slabs, final=final),
            grid=(n_ct,),
            in_specs=[
                pos_spec(0), pos_spec(1),
                tok_spec,
                pl.BlockSpec((tp, LANES), lambda i: (i, 0)),
                pl.BlockSpec((None, tp, ple_dim), lambda i, layer=layer: (layer, i, 0)),
                cs((1, d_model)), cs((ple_dim, d_model)), cs((d_model, d_model)),
                pl.BlockSpec((1, d_model), lambda i: (0, 0)) if final else _const_spec((1, d_model), layer + 1),
                pl.BlockSpec(memory_space=pl.ANY),
            ],
            out_specs=tok_spec if final else [tok_spec, tok_spec],
            out_shape=(jax.ShapeDtypeStruct((n_tok, d_model), F32) if final else
                       [jax.ShapeDtypeStruct((n_tok, d_model), F32), jax.ShapeDtypeStruct((n_tok, d_model), BF16)]),
            scratch_shapes=[pltpu.VMEM((TOP_K * tp * x_slabs, LANES), U32),
                            pltpu.VMEM((TOP_K * tp * x_slabs, LANES), U32),
                            pltpu.SemaphoreType.DMA((2,))],
            compiler_params=pltpu.CompilerParams(dimension_semantics=("arbitrary",)),
            name=f"combine_l{layer}",
        )(pos, pos, h_mid, route, p2, g3, w_ple_b, w_pg_b, gf if final else g1, ys)
        h, u_norm = (res, None) if final else res

    out = h.reshape(n_sg, seq, SEQ_GROUP, d_model).transpose(0, 2, 1, 3)
    return out.reshape(bsz, seq, d_model)
```

```python
import functools

import jax
import jax.numpy as jnp
from jax import lax
from jax.experimental import pallas as pl
from jax.experimental.pallas import tpu as pltpu

EPS = 1e-6
LRU_C = 8.0
EXPERT_BLOCK = 256
EXPERT_PAIR = 2
TOP_K = 2
GELU_C0 = 0.7978845608028654
GELU_C1 = 0.044715

LANES = 128
SUBLANES = 8
SEQ_GROUP = SUBLANES
HALO_STEPS = 16
HALO = HALO_STEPS * SEQ_GROUP
DMA_UNROLL = 8

F32 = jnp.float32
BF16 = jnp.bfloat16
U32 = jnp.uint32
I32 = jnp.int32
MIB = 1024 * 1024


def _dot(a, b):
    return jnp.dot(a, b, preferred_element_type=F32)


def _sigmoid(x):
    return 1.0 / (1.0 + jnp.exp(-x))


def _rms_norm(x, g):
    return x * lax.rsqrt(jnp.mean(x * x, axis=-1, keepdims=True) + EPS) * g


def _gelu_tanh(x):
    return 0.5 * x * (1.0 + jnp.tanh(GELU_C0 * (x + GELU_C1 * (x * x * x))))


def _softplus(x):
    return jnp.maximum(x, 0.0) + jnp.log1p(jnp.exp(-jnp.abs(x)))


def _sqrt_nonneg(x):
    return jnp.where(x > 0.0, x * lax.rsqrt(x), 0.0)


def _pack_bf16_pairs(x):
    half = x.shape[1] // 2
    bits = lax.bitcast_convert_type(x.astype(BF16).astype(F32), U32)
    return (bits[:, :half] >> 16) | (bits[:, half:] & jnp.uint32(0xFFFF0000))


def _unpack_bf16_pairs(w):
    lo = lax.bitcast_convert_type(w << 16, F32).astype(BF16)
    hi = lax.bitcast_convert_type(w & jnp.uint32(0xFFFF0000), F32).astype(BF16)
    return lo, hi


def _unpack_pairs_f32(w):
    return jnp.concatenate([lax.bitcast_convert_type(w << 16, F32),
                            lax.bitcast_convert_type(w & jnp.uint32(0xFFFF0000), F32)], axis=1)


def _store_row_slabs(ref, val, row0=0):
    n = val.shape[1] // LANES
    for c in range(n):
        ref[pl.ds(row0 * n + c, val.shape[0], stride=n), :] = val[:, c * LANES:(c + 1) * LANES]


def _load_row_slabs(ref, row0, n_rows, n):
    return jnp.concatenate([ref[pl.ds(row0 * n + c, n_rows, stride=n), :] for c in range(n)], axis=1)


def _mixer_kernel(*refs, fused, rows, n_chunks, d_model, d_lru, d_pool, n_groups, experts_per_group, slabs):
    if fused:
        (pos_cur_ref, pos_nxt_ref, hprev_ref, rprev_ref, p_ref, g3_ref, wple_ref, wpg_ref, g1_ref, ys_ref), refs = refs[:10], refs[10:]
    else:
        (u_in_ref, h_in_ref), refs = refs[:2], refs[2:]
    (win_ref, convw_ref, convb_ref, wgate_ref, brg_ref, big_ref, lam_ref, wpool_ref, pscale_ref, wupa_ref,
     wupb_ref, wout_ref, g2_ref, wr_ref, br_ref, hmid_ref, u2_ref, route_ref, counts_ref,
     z0, z1, xa_ext, xb_ext, hstate, ha_scr, hb_scr, cnt_scr), refs = refs[:27], refs[27:]
    if fused:
        h0, h1, u0, u1, ybuf0, ybuf1, sems = refs
    s = pl.program_id(0)
    steps = rows // SEQ_GROUP
    o_pool = 2 * d_lru
    o_gate = o_pool + d_pool
    c = lax.rem(jnp.maximum(s - 1, 0), n_chunks)

    @pl.when(s == 0)
    def _():
        z1[...] = jnp.zeros_like(z1)
        if fused:
            h1[...] = jnp.zeros_like(h1)

    @pl.when(s <= 1)
    def _():
        cnt_scr[...] = jnp.zeros_like(cnt_scr)

    @pl.when(c == 0)
    def _():
        xa_ext[0:HALO, :] = jnp.zeros((HALO, d_lru), F32)
        xb_ext[0:HALO, :] = jnp.zeros((HALO, d_pool), F32)
        hstate[...] = jnp.zeros_like(hstate)

    a_cols = 2 * LANES
    n_hooks = d_lru // LANES + 2
    skip_hooks = 2 if fused else 0

    def stage_b(zbuf, znext, u_ref, h_ref):
        pending = list(range(0, win_ref.shape[1], a_cols))
        hooks_left = [n_hooks]

        def stage_a_chunks():
            hooks_left[0] -= 1
            if hooks_left[0] >= n_hooks - skip_hooks:
                return
            n = -(-len(pending) // (hooks_left[0] + 1))
            for _ in range(n):
                lo = pending.pop(0)
                znext[:, lo:lo + a_cols] = _dot(u_ref[...], win_ref[:, lo:lo + a_cols])

        conv_k = convw_ref.shape[0]
        xa_pre = zbuf[:, 0:d_lru]
        xa_ext[HALO:HALO + rows, :] = xa_pre
        xa = convb_ref[...] + convw_ref[conv_k - 1:conv_k, :] * xa_pre
        for j in range(1, conv_k):
            xa = xa + convw_ref[conv_k - 1 - j:conv_k - j, :] * xa_ext[HALO - j * SEQ_GROUP:HALO - j * SEQ_GROUP + rows, :]
        xa_ext[0:HALO, :] = xa_ext[rows:rows + HALO, :]

        for p in range(d_lru // LANES):
            stage_a_chunks()
            sl = slice(p * LANES, (p + 1) * LANES)
            xa_s = xa[:, sl]
            rg = _dot(xa_s.astype(BF16), wgate_ref[p])
            r = _sigmoid(rg[:, :LANES] + brg_ref[:, sl])
            i = _sigmoid(rg[:, LANES:] + big_ref[:, sl])
            log_a = r * (-LRU_C * _softplus(-lam_ref[:, sl]))
            a = jnp.exp(log_a)
            b = _sqrt_nonneg(1.0 - a * a) * (i * xa_s)
            h = hstate[:, sl]
            hs = []
            for t in range(steps):
                rs = slice(t * SEQ_GROUP, (t + 1) * SEQ_GROUP)
                h = a[rs] * h + b[rs]
                hs.append(h)
            hstate[:, sl] = h
            hseq = jnp.concatenate(hs, axis=0)
            ha_scr[:, sl] = (hseq * _gelu_tanh(zbuf[:, d_lru + p * LANES:d_lru + (p + 1) * LANES])).astype(BF16)

        stage_a_chunks()
        xb_ext[HALO:HALO + rows, :] = zbuf[:, o_pool:o_gate]
        t_glob = (c * steps + lax.broadcasted_iota(I32, (rows, LANES), 0) // SEQ_GROUP).astype(F32)
        for g in range(d_pool // LANES):
            sl = slice(g * LANES, (g + 1) * LANES)
            window = 2 << g
            acc = xb_ext[:, sl]
            span = 1
            while span < window:
                acc = acc[span * SEQ_GROUP:] + acc[:-span * SEQ_GROUP]
                span *= 2
            acc = acc[acc.shape[0] - rows:]
            xb_s = xb_ext[HALO:HALO + rows, sl]
            pooled = acc / jnp.minimum(t_glob + 1.0, float(window)) - xb_s
            mixed = _dot(pooled.astype(BF16), wpool_ref[g]) * pscale_ref[:, sl]
            hb_scr[:, sl] = mixed.astype(BF16)
        xb_ext[0:HALO, :] = xb_ext[rows:rows + HALO, :]

        stage_a_chunks()
        assert not pending
        merged = (_sigmoid(zbuf[:, o_gate:o_gate + d_model]) * _dot(ha_scr[...], wupa_ref[...])
                  + _sigmoid(zbuf[:, o_gate + d_model:o_gate + 2 * d_model]) * _dot(hb_scr[...], wupb_ref[...]))
        h_new = h_ref[...] + _dot(merged.astype(BF16), wout_ref[...])
        hmid_ref[...] = h_new

        u2 = _rms_norm(h_new, g2_ref[...])
        _store_row_slabs(u2_ref, _pack_bf16_pairs(u2))
        logits = _dot(u2.astype(BF16), wr_ref[...]) + br_ref[...]
        lane = lax.broadcasted_iota(I32, (rows, LANES), 1).astype(F32)
        neg = -jnp.inf
        big = float(LANES)
        is_group = lane < n_groups
        glog = jnp.where(is_group, logits, neg)
        gmax = jnp.max(glog, axis=-1, keepdims=True)
        gidx = jnp.min(jnp.where(glog == gmax, lane, big), axis=-1, keepdims=True)
        psum = jnp.sum(jnp.where(is_group, jnp.exp(glog - gmax), 0.0), axis=-1, keepdims=True)
        lo = n_groups + gidx * experts_per_group
        in_group = (lane >= lo) & (lane < lo + experts_per_group)
        sel = jnp.where(in_group, logits, neg)
        m1 = jnp.max(sel, axis=-1, keepdims=True)
        i1 = jnp.min(jnp.where(sel == m1, lane, big), axis=-1, keepdims=True)
        sel2 = jnp.where(lane == i1, neg, sel)
        m2 = jnp.max(sel2, axis=-1, keepdims=True)
        i2 = jnp.min(jnp.where(sel2 == m2, lane, big), axis=-1, keepdims=True)
        e21 = jnp.exp(m2 - m1)
        w1 = 1.0 / (psum * (1.0 + e21))
        w2 = e21 * w1
        e1 = i1 - n_groups
        e2 = i2 - n_groups
        route_ref[...] = jnp.where(lane == 0, e1, jnp.where(lane == 1, e2,
                                   jnp.where(lane == 2, w1, jnp.where(lane == 3, w2, 0.0))))
        chosen = jnp.where((lane == e1) | (lane == e2), 1.0, 0.0)
        cnt_scr[...] = cnt_scr[...] + jnp.sum(chosen, axis=0, keepdims=True)
        counts_ref[...] = cnt_scr[...]

    if not fused:
        @pl.when(lax.rem(s, 2) == 0)
        def _():
            stage_b(z1, z0, u_in_ref, h_in_ref)

        @pl.when(lax.rem(s, 2) == 1)
        def _():
            stage_b(z0, z1, u_in_ref, h_in_ref)
        return

    last = pl.num_programs(0) - 1

    def gather_row(pos_ref, buf, q, t, k):
        src = pl.multiple_of(pos_ref[k, t] * slabs, slabs)
        return pltpu.make_async_copy(ys_ref.at[pl.ds(src, slabs)], buf.at[pl.ds((k * rows + t) * slabs, slabs)],
                                     sems.at[q])

    def wait_tile(buf, q):
        for k in range(TOP_K):
            pltpu.make_async_copy(ys_ref.at[pl.ds(0, rows * slabs)], buf.at[pl.ds(k * rows * slabs, rows * slabs)],
                                  sems.at[q]).wait()

    @pl.when(s == 0)
    def _():
        def body(i, carry):
            for uu in range(DMA_UNROLL):
                for k in range(TOP_K):
                    gather_row(pos_cur_ref, ybuf0, 0, i * DMA_UNROLL + uu, k).start(priority=k)
            return carry
        lax.fori_loop(0, rows // DMA_UNROLL, body, 0)

    def fused_step(q, zcur, znext, hcur, hprev, ucur, buf, nbuf):
        wait_tile(buf, q)
        for t in range(rows):
            for k in range(TOP_K):
                gather_row(pos_nxt_ref, nbuf, 1 - q, t, k).start(priority=k)
        h3 = _ple_update(hprev_ref, rprev_ref, p_ref, g3_ref, wple_ref, wpg_ref, buf, rows, slabs)
        hcur[...] = h3
        ucur[...] = _rms_norm(h3, g1_ref[...]).astype(BF16)
        stage_b(zcur, znext, ucur, hprev)

        @pl.when(s == last)
        def _():
            wait_tile(nbuf, 1 - q)

    @pl.when(lax.rem(s, 2) == 0)
    def _():
        fused_step(0, z1, z0, h0, h1, u0, ybuf0, ybuf1)

    @pl.when(lax.rem(s, 2) == 1)
    def _():
        fused_step(1, z0, z1, h1, h0, u1, ybuf1, ybuf0)


def _ple_update(hmid_ref, route_ref, p_ref, g3_ref, wple_ref, wpg_ref, buf, tp, slabs):
    r = route_ref[...]
    y0 = _unpack_pairs_f32(_load_row_slabs(buf, 0, tp, slabs))
    y1 = _unpack_pairs_f32(_load_row_slabs(buf, tp, tp, slabs))
    h2 = hmid_ref[...] + r[:, 2:3] * y0 + r[:, 3:4] * y1
    u3 = _rms_norm(h2, g3_ref[...]).astype(BF16)
    gate = _sigmoid(_dot(u3, wpg_ref[...]))
    return h2 + gate * _dot(p_ref[...].astype(BF16), wple_ref[...])


def _prenorm_kernel(h_ref, g_ref, u_ref):
    u_ref[...] = _rms_norm(h_ref[...], g_ref[...]).astype(BF16)


def _rank_kernel(route_ref, counts_ref, pos_ref, tri_scr, run_scr, start_scr, *, tr, n_experts):
    i = pl.program_id(0)

    @pl.when(i == 0)
    def _():
        r_i = lax.broadcasted_iota(I32, (tr, tr), 0)
        c_i = lax.broadcasted_iota(I32, (tr, tr), 1)
        tri_scr[...] = jnp.where(r_i > c_i, 1.0, 0.0).astype(BF16)
        run_scr[...] = jnp.zeros_like(run_scr)
        cnt = counts_ref[...].astype(I32)
        padded = ((cnt + (EXPERT_BLOCK - 1)) // EXPERT_BLOCK) * EXPERT_BLOCK
        lane8 = lax.broadcasted_iota(I32, (SUBLANES, LANES), 1)
        incl = padded
        k = 1
        while k < n_experts:
            incl = incl + jnp.where(lane8 >= k, pltpu.roll(incl, k, 1), 0)
            k *= 2
        start_scr[...] = (incl - padded).astype(F32)

    r = route_ref[...]
    lane = lax.broadcasted_iota(I32, (tr, LANES), 1).astype(F32)
    oh0 = lane == r[:, 0:1]
    oh1 = lane == r[:, 1:2]
    oh = jnp.where(oh0 | oh1, 1.0, 0.0)
    before = _dot(tri_scr[...], oh.astype(BF16)) + (run_scr[0:1, :] + start_scr[0:1, :])
    rk0 = jnp.sum(jnp.where(oh0, before, 0.0), axis=-1, keepdims=True)
    rk1 = jnp.sum(jnp.where(oh1, before, 0.0), axis=-1, keepdims=True)
    cols = jnp.where(lane == 0, rk0, jnp.where(lane == 1, rk1, 0.0))
    pos_ref[...] = jnp.transpose(cols)[0:SUBLANES, :].astype(I32)
    run_scr[...] = run_scr[...] + jnp.sum(oh, axis=0, keepdims=True)


def _dispatch_kernel(cnt_ref, end_ref, pos_ref, u2_ref, xs_ref, zero_scr, sem, zsem, *, td, slabs, n_experts):
    @pl.when(pl.program_id(0) == 0)
    def _():
        zero_scr[...] = jnp.zeros_like(zero_scr)
        n_blocks = xs_ref.shape[0] // (EXPERT_BLOCK * slabs)
        n_used = end_ref[n_experts - 1] // EXPERT_BLOCK

        def zero_block(blk):
            row0 = pl.multiple_of(blk * (EXPERT_BLOCK * slabs), EXPERT_BLOCK * slabs)
            return pltpu.make_async_copy(zero_scr, xs_ref.at[pl.ds(row0, EXPERT_BLOCK * slabs)], zsem)

        todo = [(cnt_ref[e] > 0, end_ref[e] // EXPERT_BLOCK - 1) for e in range(n_experts)]
        todo += [(blk >= n_used, blk) for blk in range(max(n_blocks - n_experts - EXPERT_PAIR + 1, 0), n_blocks)]
        for cond, blk in todo:
            @pl.when(cond)
            def _():
                zero_block(blk).start()
        for cond, blk in todo:
            @pl.when(cond)
            def _():
                zero_block(blk).wait()

    def issue(q, carry):
        for uu in range(DMA_UNROLL):
            t = q * DMA_UNROLL + uu
            for k in range(TOP_K):
                dst = pl.multiple_of(pos_ref[k, t] * slabs, slabs)
                pltpu.make_async_copy(u2_ref.at[pl.ds(t * slabs, slabs)], xs_ref.at[pl.ds(dst, slabs)],
                                      sem).start(priority=k)
        return carry

    lax.fori_loop(0, td // DMA_UNROLL, issue, 0)
    for _ in range(TOP_K):
        pltpu.make_async_copy(u2_ref, xs_ref.at[pl.ds(0, td * slabs)], sem).wait()


def _expert_kernel(be_ref, nu_ref, xs_ref, *rest, slabs):
    w_f32 = (rest[0:3], rest[3:6])
    ys_ref = rest[6]
    w_bf16 = (rest[7:10], rest[10:13])
    i = pl.program_id(0)
    half = slabs * LANES
    blk_rows = EXPERT_BLOCK * slabs

    for b in range(EXPERT_PAIR):
        j = EXPERT_PAIR * i + b

        @pl.when((i == 0) | (be_ref[j] != be_ref[jnp.maximum(j - EXPERT_PAIR, 0)]))
        def _():
            for dst, src in zip(w_bf16[b], w_f32[b]):
                dst[...] = src[...].astype(BF16)

    def ffn(b):
        wg_b, wu_b, wd_b = w_bf16[b]
        lo, hi = _unpack_bf16_pairs(_load_row_slabs(xs_ref, b * EXPERT_BLOCK, EXPERT_BLOCK, slabs))
        gate = _dot(lo, wg_b[0:half, :]) + _dot(hi, wg_b[half:, :])
        up = _dot(lo, wu_b[0:half, :]) + _dot(hi, wu_b[half:, :])
        act = (gate * _sigmoid(gate)) * up
        _store_row_slabs(ys_ref, _pack_bf16_pairs(_dot(act.astype(BF16), wd_b[...])), b * EXPERT_BLOCK)

    def zero(b):
        ys_ref[pl.ds(b * blk_rows, blk_rows), :] = jnp.zeros((blk_rows, LANES), U32)

    n_valid = nu_ref[0] - EXPERT_PAIR * i

    @pl.when(n_valid >= 2)
    def _():
        ffn(0)
        ffn(1)

    @pl.when(n_valid == 1)
    def _():
        ffn(0)
        zero(1)

    @pl.when(n_valid <= 0)
    def _():
        zero(0)
        zero(1)


def _combine_kernel(pos_cur_ref, pos_nxt_ref, hmid_ref, route_ref, p_ref, g3_ref, wple_ref, wpg_ref, gf_ref,
                    ys_ref, out_ref, ybuf0, ybuf1, sems, *, tp, slabs):
    i = pl.program_id(0)
    n = pl.num_programs(0)

    def gather_row(pos_ref, buf, s, t, k):
        src = pl.multiple_of(pos_ref[k, t] * slabs, slabs)
        return pltpu.make_async_copy(ys_ref.at[pl.ds(src, slabs)], buf.at[pl.ds((k * tp + t) * slabs, slabs)],
                                     sems.at[s])

    def wait_tile(buf, s):
        for k in range(TOP_K):
            pltpu.make_async_copy(ys_ref.at[pl.ds(0, tp * slabs)], buf.at[pl.ds(k * tp * slabs, tp * slabs)],
                                  sems.at[s]).wait()

    @pl.when(i == 0)
    def _():
        def body(q, carry):
            for uu in range(DMA_UNROLL):
                for k in range(TOP_K):
                    gather_row(pos_cur_ref, ybuf0, 0, q * DMA_UNROLL + uu, k).start(priority=k)
            return carry
        lax.fori_loop(0, tp // DMA_UNROLL, body, 0)

    def step(buf, s, nbuf, ns):
        wait_tile(buf, s)
        for t in range(tp):
            for k in range(TOP_K):
                gather_row(pos_nxt_ref, nbuf, ns, t, k).start(priority=k)
        h3 = _ple_update(hmid_ref, route_ref, p_ref, g3_ref, wple_ref, wpg_ref, buf, tp, slabs)
        out_ref[...] = _rms_norm(h3, gf_ref[...])

        @pl.when(i == n - 1)
        def _():
            wait_tile(nbuf, ns)

    @pl.when(lax.rem(i, 2) == 0)
    def _():
        step(ybuf0, 0, ybuf1, 1)

    @pl.when(lax.rem(i, 2) == 1)
    def _():
        step(ybuf1, 1, ybuf0, 0)


def _const_spec(shape, layer):
    nd = len(shape)
    return pl.BlockSpec((None,) + tuple(shape), lambda i: (layer,) + (0,) * nd, pipeline_mode=pl.Buffered(1))


def _block_diag_gates(w_r, w_i):
    n_layers, n_heads, hd, _ = w_r.shape
    per = LANES // hd
    n_slab = n_heads // per

    def bd(w):
        w = w.reshape(n_layers, n_slab, per, hd, hd)
        eye = jnp.eye(per, dtype=w.dtype)
        out = jnp.einsum('lspij,pq->lspiqj', w, eye)
        return out.reshape(n_layers, n_slab, LANES, LANES)

    return jnp.concatenate([bd(w_r), bd(w_i)], axis=-1).astype(BF16)


def _tile_sizes(seq, n_tok):
    mixer_rows = min(256, seq * SEQ_GROUP)
    rank_rows = min(512, n_tok)
    combine_rows = min(256, n_tok)
    assert (seq * SEQ_GROUP) % mixer_rows == 0 and n_tok % rank_rows == 0 and rank_rows % combine_rows == 0
    assert mixer_rows % SEQ_GROUP == 0 and rank_rows % DMA_UNROLL == 0 and combine_rows % DMA_UNROLL == 0
    return mixer_rows, rank_rows, combine_rows


def kernel(x, p, norm_mix_g, w_in, conv_w, conv_b, w_rg, b_rg, w_ig, b_ig, lru_lambda, w_pool, pool_scale, w_up_a, w_up_b, w_out, norm_ffn_g, w_router_group, b_router_group, w_router_expert, b_router_expert, w_gate_e, w_up_e, w_down_e, norm_ple_g, w_ple, w_ple_gate, final_norm_g):
    bsz, seq, d_model = x.shape
    n_layers, _, d_in = w_in.shape
    d_lru = conv_w.shape[2]
    d_pool = pool_scale.shape[1]
    n_groups = w_router_group.shape[2]
    n_experts = w_router_expert.shape[2]
    experts_per_group = n_experts // n_groups
    d_expert = w_gate_e.shape[3]
    ple_dim = w_ple.shape[1]
    n_tok = bsz * seq
    half = d_model // 2
    x_slabs = half // LANES

    rows_m, tr, tp = _tile_sizes(seq, n_tok)
    assert tp == rows_m
    assert bsz % SEQ_GROUP == 0
    assert d_in == 2 * d_lru + d_pool + 2 * d_model
    assert LANES % w_rg.shape[2] == 0 and d_lru % LANES == 0 and d_model % (2 * LANES) == 0
    assert w_pool.shape[2] == LANES and (2 << (w_pool.shape[1] - 1)) <= HALO_STEPS and conv_w.shape[1] - 1 <= HALO_STEPS
    assert n_groups + n_experts <= LANES

    n_blocks = (n_tok * TOP_K + n_experts * (EXPERT_BLOCK - 1) + EXPERT_BLOCK - 1) // EXPERT_BLOCK
    n_blocks = -(-n_blocks // EXPERT_PAIR) * EXPERT_PAIR
    n_rows = n_blocks * EXPERT_BLOCK

    w_in_b = w_in.astype(BF16)
    w_gate_bd = _block_diag_gates(w_rg, w_ig)
    w_pool_b = w_pool.astype(BF16)
    w_up_a_b = w_up_a.astype(BF16)
    w_up_b_b = w_up_b.astype(BF16)
    w_out_b = w_out.astype(BF16)
    w_router = jnp.concatenate([w_router_group, w_router_expert], axis=-1)
    w_router = jnp.pad(w_router, ((0, 0), (0, 0), (0, LANES - w_router.shape[-1]))).astype(BF16)
    b_router = jnp.concatenate([b_router_group, b_router_expert], axis=-1)
    b_router = jnp.pad(b_router, ((0, 0), (0, LANES - b_router.shape[-1])))[:, None, :]
    w_ple_b = w_ple.astype(BF16)
    w_pg_b = w_ple_gate.astype(BF16)
    row3 = lambda a: a[:, None, :]
    g1, g2, g3 = row3(norm_mix_g), row3(norm_ffn_g), row3(norm_ple_g)
    conv_b3, b_rg3, b_ig3, lam3, pscale3 = row3(conv_b), row3(b_rg), row3(b_ig), row3(lru_lambda), row3(pool_scale)
    gf = final_norm_g[None, :]

    n_sg = bsz // SEQ_GROUP
    h = x.reshape(n_sg, SEQ_GROUP, seq, d_model).transpose(0, 2, 1, 3).reshape(n_tok, d_model)
    p2 = p.reshape(n_layers, n_sg, SEQ_GROUP, seq, ple_dim).transpose(0, 1, 3, 2, 4).reshape(n_layers, n_tok, ple_dim)
    n_chunks = seq * SEQ_GROUP // rows_m
    n_mt = n_tok // rows_m
    n_rt = n_tok // tr
    per_rt = tr // tp
    n_ct = n_tok // tp

    t_pre = min(2048, n_tok)
    assert n_tok % t_pre == 0
    u_norm = pl.pallas_call(
        _prenorm_kernel,
        grid=(n_tok // t_pre,),
        in_specs=[pl.BlockSpec((t_pre, d_model), lambda i: (i, 0)), _const_spec((1, d_model), 0)],
        out_specs=pl.BlockSpec((t_pre, d_model), lambda i: (i, 0)),
        out_shape=jax.ShapeDtypeStruct((n_tok, d_model), BF16),
        compiler_params=pltpu.CompilerParams(dimension_semantics=("arbitrary",)),
        name="prenorm",
    )(h, g1)

    def pos_spec(shift, n_tiles):
        return pl.BlockSpec(
            (None, SUBLANES, tp),
            lambda i: (jnp.minimum(i + shift, n_tiles - 1) // per_rt, 0, jnp.minimum(i + shift, n_tiles - 1) % per_rt),
            memory_space=pltpu.SMEM)

    for layer in range(n_layers):
        cs = functools.partial(_const_spec, layer=layer)
        in_map = lambda s: (jnp.minimum(s, n_mt - 1), 0)
        out_map = lambda s: (jnp.maximum(s - 1, 0), 0)
        fused = layer > 0
        if fused:
            csp = functools.partial(_const_spec, layer=layer - 1)
            head_specs = [
                pos_spec(0, n_mt), pos_spec(1, n_mt),
                pl.BlockSpec((rows_m, d_model), in_map),
                pl.BlockSpec((rows_m, LANES), in_map),
                pl.BlockSpec((None, rows_m, ple_dim), lambda s, layer=layer: (layer - 1, jnp.minimum(s, n_mt - 1), 0)),
                csp((1, d_model)), csp((ple_dim, d_model)), csp((d_model, d_model)), cs((1, d_model)),
                pl.BlockSpec(memory_space=pl.ANY),
            ]
            head_args = (pos, pos, h_mid, route, p2, g3, w_ple_b, w_pg_b, g1, ys)
            head_scratch = [
                pltpu.VMEM((rows_m, d_model), F32), pltpu.VMEM((rows_m, d_model), F32),
                pltpu.VMEM((rows_m, d_model), BF16), pltpu.VMEM((rows_m, d_model), BF16),
                pltpu.VMEM((TOP_K * rows_m * x_slabs, LANES), U32), pltpu.VMEM((TOP_K * rows_m * x_slabs, LANES), U32),
                pltpu.SemaphoreType.DMA((2,)),
            ]
        else:
            head_specs = [pl.BlockSpec((rows_m, d_model), in_map), pl.BlockSpec((rows_m, d_model), out_map)]
            head_args = (u_norm, h)
            head_scratch = []
        h_mid, u2, route, counts = pl.pallas_call(
            functools.partial(_mixer_kernel, fused=fused, rows=rows_m, n_chunks=n_chunks, d_model=d_model,
                              d_lru=d_lru, d_pool=d_pool, n_groups=n_groups, experts_per_group=experts_per_group,
                              slabs=x_slabs),
            grid=(n_mt + 1,),
            in_specs=head_specs + [
                cs((d_model, d_in)), cs(conv_w.shape[1:]), cs((1, d_lru)),
                cs(w_gate_bd.shape[1:]), cs((1, d_lru)), cs((1, d_lru)), cs((1, d_lru)),
                cs(w_pool.shape[1:]), cs((1, d_pool)), cs((d_lru, d_model)), cs((d_pool, d_model)),
                cs((d_model, d_model)), cs((1, d_model)), cs((d_model, LANES)), cs((1, LANES)),
            ],
            out_specs=[
                pl.BlockSpec((rows_m, d_model), out_map),
                pl.BlockSpec((rows_m * x_slabs, LANES), out_map),
                pl.BlockSpec((rows_m, LANES), out_map),
                pl.BlockSpec((SUBLANES, LANES), lambda s: (0, 0)),
            ],
            out_shape=[
                jax.ShapeDtypeStruct((n_tok, d_model), F32),
                jax.ShapeDtypeStruct((n_tok * x_slabs, LANES), U32),
                jax.ShapeDtypeStruct((n_tok, LANES), F32),
                jax.ShapeDtypeStruct((SUBLANES, LANES), F32),
            ],
            scratch_shapes=[
                pltpu.VMEM((rows_m, d_in), F32), pltpu.VMEM((rows_m, d_in), F32),
                pltpu.VMEM((HALO + rows_m, d_lru), F32),
                pltpu.VMEM((HALO + rows_m, d_pool), F32),
                pltpu.VMEM((SEQ_GROUP, d_lru), F32),
                pltpu.VMEM((rows_m, d_lru), BF16),
                pltpu.VMEM((rows_m, d_pool), BF16),
                pltpu.VMEM((SUBLANES, LANES), F32),
            ] + head_scratch,
            compiler_params=pltpu.CompilerParams(dimension_semantics=("arbitrary",), vmem_limit_bytes=58 * MIB),
            name=f"mixer_l{layer}",
        )(*head_args, w_in_b, conv_w, conv_b3, w_gate_bd, b_rg3, b_ig3, lam3, w_pool_b, pscale3,
          w_up_a_b, w_up_b_b, w_out_b, g2, w_router, b_router)

        pos = pl.pallas_call(
            functools.partial(_rank_kernel, tr=tr, n_experts=n_experts),
            grid=(n_rt,),
            in_specs=[pl.BlockSpec((tr, LANES), lambda i: (i, 0)),
                      pl.BlockSpec((SUBLANES, LANES), lambda i: (0, 0))],
            out_specs=pl.BlockSpec((None, SUBLANES, tr), lambda i: (i, 0, 0)),
            out_shape=jax.ShapeDtypeStruct((n_rt, SUBLANES, tr), I32),
            scratch_shapes=[pltpu.VMEM((tr, tr), BF16), pltpu.VMEM((SUBLANES, LANES), F32),
                            pltpu.VMEM((SUBLANES, LANES), F32)],
            compiler_params=pltpu.CompilerParams(dimension_semantics=("arbitrary",)),
            name=f"rank_l{layer}",
        )(route, counts)

        cnt = counts[0, :n_experts].astype(I32)
        pad_end = jnp.cumsum(((cnt + EXPERT_BLOCK - 1) // EXPERT_BLOCK) * EXPERT_BLOCK)
        n_used = (pad_end[-1] // EXPERT_BLOCK).astype(I32)
        block_row0 = jnp.minimum(jnp.arange(n_blocks, dtype=I32), n_used - 1) * EXPERT_BLOCK
        block_expert = jnp.minimum(jnp.sum(pad_end[None, :] <= block_row0[:, None], axis=1), n_experts - 1).astype(I32)

        xs = pl.pallas_call(
            functools.partial(_dispatch_kernel, td=tr, slabs=x_slabs, n_experts=n_experts),
            grid_spec=pltpu.PrefetchScalarGridSpec(
                num_scalar_prefetch=2,
                grid=(n_rt,),
                in_specs=[
                    pl.BlockSpec((None, SUBLANES, tr), lambda i, cnt, end: (i, 0, 0), memory_space=pltpu.SMEM),
                    pl.BlockSpec((tr * x_slabs, LANES), lambda i, cnt, end: (i, 0)),
                ],
                out_specs=pl.BlockSpec(memory_space=pl.ANY),
                scratch_shapes=[pltpu.VMEM((EXPERT_BLOCK * x_slabs, LANES), U32),
                                pltpu.SemaphoreType.DMA, pltpu.SemaphoreType.DMA],
            ),
            out_shape=jax.ShapeDtypeStruct((n_rows * x_slabs, LANES), U32),
            compiler_params=pltpu.CompilerParams(dimension_semantics=("arbitrary",), has_side_effects=True),
            name=f"dispatch_l{layer}",
        )(cnt, pad_end, pos, u2)

        pair_rows = EXPERT_PAIR * EXPERT_BLOCK * x_slabs
        used_map = lambda i, be, nu: (jnp.minimum(i, (nu[0] - 1) // EXPERT_PAIR), 0)
        w_specs, w_scratch = [], []
        for b in range(EXPERT_PAIR):
            w_map = lambda i, be, nu, layer=layer, b=b: (layer, be[EXPERT_PAIR * i + b], 0, 0)
            for shape in ((d_model, d_expert), (d_model, d_expert), (d_expert, d_model)):
                w_specs.append(pl.BlockSpec((None, None) + shape, w_map))
                w_scratch.append(pltpu.VMEM(shape, BF16))
        ys = pl.pallas_call(
            functools.partial(_expert_kernel, slabs=x_slabs),
            grid_spec=pltpu.PrefetchScalarGridSpec(
                num_scalar_prefetch=2,
                grid=(n_blocks // EXPERT_PAIR,),
                in_specs=[pl.BlockSpec((pair_rows, LANES), used_map)] + w_specs,
                out_specs=pl.BlockSpec((pair_rows, LANES), lambda i, be, nu: (i, 0)),
                scratch_shapes=w_scratch,
            ),
            out_shape=jax.ShapeDtypeStruct((n_rows * x_slabs, LANES), U32),
            compiler_params=pltpu.CompilerParams(dimension_semantics=("arbitrary",), vmem_limit_bytes=48 * MIB),
            name=f"experts_l{layer}",
        )(block_expert, n_used[None], xs, *([w_gate_e, w_up_e, w_down_e] * EXPERT_PAIR))

    last = n_layers - 1
    csl = functools.partial(_const_spec, layer=last)
    tok_spec = pl.BlockSpec((tp, d_model), lambda i: (i, 0))
    h = pl.pallas_call(
        functools.partial(_combine_kernel, tp=tp, slabs=x_slabs),
        grid=(n_ct,),
        in_specs=[
            pos_spec(0, n_ct), pos_spec(1, n_ct),
            tok_spec,
            pl.BlockSpec((tp, LANES), lambda i: (i, 0)),
            pl.BlockSpec((None, tp, ple_dim), lambda i: (last, i, 0)),
            csl((1, d_model)), csl((ple_dim, d_model)), csl((d_model, d_model)),
            pl.BlockSpec((1, d_model), lambda i: (0, 0)),
            pl.BlockSpec(memory_space=pl.ANY),
        ],
        out_specs=tok_spec,
        out_shape=jax.ShapeDtypeStruct((n_tok, d_model), F32),
        scratch_shapes=[pltpu.VMEM((TOP_K * tp * x_slabs, LANES), U32),
                        pltpu.VMEM((TOP_K * tp * x_slabs, LANES), U32),
                        pltpu.SemaphoreType.DMA((2,))],
        compiler_params=pltpu.CompilerParams(dimension_semantics=("arbitrary",)),
        name="combine_final",
    )(pos, pos, h_mid, route, p2, g3, w_ple_b, w_pg_b, gf, ys)

    out = h.reshape(n_sg, seq, SEQ_GROUP, d_model).transpose(0, 2, 1, 3)
    return out.reshape(bsz, seq, d_model)
```

```python
import functools

import jax
import jax.numpy as jnp
from jax import lax
from jax.experimental import pallas as pl
from jax.experimental.pallas import tpu as pltpu

EPS = 1e-6
LRU_C = 8.0
EXPERT_BLOCK = 256
EXPERT_PAIR = 2
TOP_K = 2
GELU_C0 = 0.7978845608028654
GELU_C1 = 0.044715

LANES = 128
SUBLANES = 8
SEQ_GROUP = SUBLANES
HALO_STEPS = 16
HALO = HALO_STEPS * SEQ_GROUP
DMA_UNROLL = 8

F32 = jnp.float32
BF16 = jnp.bfloat16
U32 = jnp.uint32
I32 = jnp.int32
MIB = 1024 * 1024


def _dot(a, b):
    return jnp.dot(a, b, preferred_element_type=F32)


def _sigmoid(x):
    return 1.0 / (1.0 + jnp.exp(-x))


def _rms_norm(x, g):
    return x * lax.rsqrt(jnp.mean(x * x, axis=-1, keepdims=True) + EPS) * g


def _gelu_tanh(x):
    return 0.5 * x * (1.0 + jnp.tanh(GELU_C0 * (x + GELU_C1 * (x * x * x))))


def _softplus(x):
    return jnp.maximum(x, 0.0) + jnp.log1p(jnp.exp(-jnp.abs(x)))


def _sqrt_nonneg(x):
    return jnp.where(x > 0.0, x * lax.rsqrt(x), 0.0)


def _pack_bf16_pairs(x):
    half = x.shape[1] // 2
    bits = lax.bitcast_convert_type(x.astype(BF16).astype(F32), U32)
    return (bits[:, :half] >> 16) | (bits[:, half:] & jnp.uint32(0xFFFF0000))


def _unpack_bf16_pairs(w):
    lo = lax.bitcast_convert_type(w << 16, F32).astype(BF16)
    hi = lax.bitcast_convert_type(w & jnp.uint32(0xFFFF0000), F32).astype(BF16)
    return lo, hi


def _unpack_pairs_f32(w):
    return jnp.concatenate([lax.bitcast_convert_type(w << 16, F32),
                            lax.bitcast_convert_type(w & jnp.uint32(0xFFFF0000), F32)], axis=1)


def _store_row_slabs(ref, val, row0=0):
    n = val.shape[1] // LANES
    for c in range(n):
        ref[pl.ds(row0 * n + c, val.shape[0], stride=n), :] = val[:, c * LANES:(c + 1) * LANES]


def _load_row_slabs(ref, row0, n_rows, n):
    return jnp.concatenate([ref[pl.ds(row0 * n + c, n_rows, stride=n), :] for c in range(n)], axis=1)


def _mixer_kernel(*refs, fused, rows, n_chunks, d_model, d_lru, d_pool, n_groups, experts_per_group, slabs):
    if fused:
        (pos_cur_ref, pos_nxt_ref, hprev_ref, rprev_ref, p_ref, g3_ref, wple_ref, wpg_ref, g1_ref, ys_ref), refs = refs[:10], refs[10:]
    else:
        (h_nxt_ref, h_in_ref, g1_ref), refs = refs[:3], refs[3:]
    (win_ref, convw_ref, convb_ref, wgate_ref, brg_ref, big_ref, lam_ref, wpool_ref, pscale_ref, wupa_ref,
     wupb_ref, wout_ref, g2_ref, wr_ref, br_ref, hmid_ref, u2_ref, route_ref, counts_ref,
     z0, z1, xa_ext, xb_ext, hstate, ha_scr, hb_scr, cnt_scr), refs = refs[:27], refs[27:]
    if fused:
        h0, h1, u0, u1, ybuf0, ybuf1, sems = refs
    else:
        u0, u1 = refs
    s = pl.program_id(0)
    steps = rows // SEQ_GROUP
    o_pool = 2 * d_lru
    o_gate = o_pool + d_pool
    c = lax.rem(jnp.maximum(s - 1, 0), n_chunks)

    @pl.when(s == 0)
    def _():
        z1[...] = jnp.zeros_like(z1)
        if fused:
            h1[...] = jnp.zeros_like(h1)

    @pl.when(s <= 1)
    def _():
        cnt_scr[...] = jnp.zeros_like(cnt_scr)

    @pl.when(c == 0)
    def _():
        xa_ext[0:HALO, :] = jnp.zeros((HALO, d_lru), F32)
        xb_ext[0:HALO, :] = jnp.zeros((HALO, d_pool), F32)
        hstate[...] = jnp.zeros_like(hstate)

    a_cols = 2 * LANES
    n_hooks = d_lru // LANES + 2
    skip_hooks = 2 if fused else 1

    def stage_b(zbuf, znext, u_ref, h_ref):
        pending = list(range(0, win_ref.shape[1], a_cols))
        hooks_left = [n_hooks]

        def stage_a_chunks():
            hooks_left[0] -= 1
            if hooks_left[0] >= n_hooks - skip_hooks:
                return
            n = -(-len(pending) // (hooks_left[0] + 1))
            for _ in range(n):
                lo = pending.pop(0)
                znext[:, lo:lo + a_cols] = _dot(u_ref[...], win_ref[:, lo:lo + a_cols])

        conv_k = convw_ref.shape[0]
        xa_pre = zbuf[:, 0:d_lru]
        xa_ext[HALO:HALO + rows, :] = xa_pre
        xa = convb_ref[...] + convw_ref[conv_k - 1:conv_k, :] * xa_pre
        for j in range(1, conv_k):
            xa = xa + convw_ref[conv_k - 1 - j:conv_k - j, :] * xa_ext[HALO - j * SEQ_GROUP:HALO - j * SEQ_GROUP + rows, :]
        xa_ext[0:HALO, :] = xa_ext[rows:rows + HALO, :]

        for p in range(d_lru // LANES):
            stage_a_chunks()
            sl = slice(p * LANES, (p + 1) * LANES)
            xa_s = xa[:, sl]
            rg = _dot(xa_s.astype(BF16), wgate_ref[p])
            r = _sigmoid(rg[:, :LANES] + brg_ref[:, sl])
            i = _sigmoid(rg[:, LANES:] + big_ref[:, sl])
            log_a = r * (-LRU_C * _softplus(-lam_ref[:, sl]))
            a = jnp.exp(log_a)
            b = _sqrt_nonneg(1.0 - a * a) * (i * xa_s)
            h = hstate[:, sl]
            hs = []
            for t in range(steps):
                rs = slice(t * SEQ_GROUP, (t + 1) * SEQ_GROUP)
                h = a[rs] * h + b[rs]
                hs.append(h)
            hstate[:, sl] = h
            hseq = jnp.concatenate(hs, axis=0)
            ha_scr[:, sl] = (hseq * _gelu_tanh(zbuf[:, d_lru + p * LANES:d_lru + (p + 1) * LANES])).astype(BF16)

        stage_a_chunks()
        xb_ext[HALO:HALO + rows, :] = zbuf[:, o_pool:o_gate]
        t_glob = (c * steps + lax.broadcasted_iota(I32, (rows, LANES), 0) // SEQ_GROUP).astype(F32)
        for g in range(d_pool // LANES):
            sl = slice(g * LANES, (g + 1) * LANES)
            window = 2 << g
            acc = xb_ext[:, sl]
            span = 1
            while span < window:
                acc = acc[span * SEQ_GROUP:] + acc[:-span * SEQ_GROUP]
                span *= 2
            acc = acc[acc.shape[0] - rows:]
            xb_s = xb_ext[HALO:HALO + rows, sl]
            pooled = acc / jnp.minimum(t_glob + 1.0, float(window)) - xb_s
            mixed = _dot(pooled.astype(BF16), wpool_ref[g]) * pscale_ref[:, sl]
            hb_scr[:, sl] = mixed.astype(BF16)
        xb_ext[0:HALO, :] = xb_ext[rows:rows + HALO, :]

        stage_a_chunks()
        assert not pending
        merged = (_sigmoid(zbuf[:, o_gate:o_gate + d_model]) * _dot(ha_scr[...], wupa_ref[...])
                  + _sigmoid(zbuf[:, o_gate + d_model:o_gate + 2 * d_model]) * _dot(hb_scr[...], wupb_ref[...]))
        h_new = h_ref[...] + _dot(merged.astype(BF16), wout_ref[...])
        hmid_ref[...] = h_new

        u2 = _rms_norm(h_new, g2_ref[...])
        _store_row_slabs(u2_ref, _pack_bf16_pairs(u2))
        logits = _dot(u2.astype(BF16), wr_ref[...]) + br_ref[...]
        lane = lax.broadcasted_iota(I32, (rows, LANES), 1).astype(F32)
        neg = -jnp.inf
        big = float(LANES)
        is_group = lane < n_groups
        glog = jnp.where(is_group, logits, neg)
        gmax = jnp.max(glog, axis=-1, keepdims=True)
        gidx = jnp.min(jnp.where(glog == gmax, lane, big), axis=-1, keepdims=True)
        psum = jnp.sum(jnp.where(is_group, jnp.exp(glog - gmax), 0.0), axis=-1, keepdims=True)
        lo = n_groups + gidx * experts_per_group
        in_group = (lane >= lo) & (lane < lo + experts_per_group)
        sel = jnp.where(in_group, logits, neg)
        m1 = jnp.max(sel, axis=-1, keepdims=True)
        i1 = jnp.min(jnp.where(sel == m1, lane, big), axis=-1, keepdims=True)
        sel2 = jnp.where(lane == i1, neg, sel)
        m2 = jnp.max(sel2, axis=-1, keepdims=True)
        i2 = jnp.min(jnp.where(sel2 == m2, lane, big), axis=-1, keepdims=True)
        e21 = jnp.exp(m2 - m1)
        w1 = 1.0 / (psum * (1.0 + e21))
        w2 = e21 * w1
        e1 = i1 - n_groups
        e2 = i2 - n_groups
        route_ref[...] = jnp.where(lane == 0, e1, jnp.where(lane == 1, e2,
                                   jnp.where(lane == 2, w1, jnp.where(lane == 3, w2, 0.0))))
        chosen = jnp.where((lane == e1) | (lane == e2), 1.0, 0.0)
        cnt_scr[...] = cnt_scr[...] + jnp.sum(chosen, axis=0, keepdims=True)
        counts_ref[...] = cnt_scr[...]

    if not fused:
        @pl.when(lax.rem(s, 2) == 0)
        def _():
            u0[...] = _rms_norm(h_nxt_ref[...], g1_ref[...]).astype(BF16)
            stage_b(z1, z0, u0, h_in_ref)

        @pl.when(lax.rem(s, 2) == 1)
        def _():
            u1[...] = _rms_norm(h_nxt_ref[...], g1_ref[...]).astype(BF16)
            stage_b(z0, z1, u1, h_in_ref)
        return

    last = pl.num_programs(0) - 1

    def gather_row(pos_ref, buf, q, t, k):
        src = pl.multiple_of(pos_ref[k, t] * slabs, slabs)
        return pltpu.make_async_copy(ys_ref.at[pl.ds(src, slabs)], buf.at[pl.ds((k * rows + t) * slabs, slabs)],
                                     sems.at[q])

    def wait_tile(buf, q):
        for k in range(TOP_K):
            pltpu.make_async_copy(ys_ref.at[pl.ds(0, rows * slabs)], buf.at[pl.ds(k * rows * slabs, rows * slabs)],
                                  sems.at[q]).wait()

    @pl.when(s == 0)
    def _():
        def body(i, carry):
            for uu in range(DMA_UNROLL):
                for k in range(TOP_K):
                    gather_row(pos_cur_ref, ybuf0, 0, i * DMA_UNROLL + uu, k).start(priority=k)
            return carry
        lax.fori_loop(0, rows // DMA_UNROLL, body, 0)

    def fused_step(q, zcur, znext, hcur, hprev, ucur, buf, nbuf):
        wait_tile(buf, q)
        for t in range(rows):
            for k in range(TOP_K):
                gather_row(pos_nxt_ref, nbuf, 1 - q, t, k).start(priority=k)
        h3 = _ple_update(hprev_ref, rprev_ref, p_ref, g3_ref, wple_ref, wpg_ref, buf, rows, slabs)
        hcur[...] = h3
        ucur[...] = _rms_norm(h3, g1_ref[...]).astype(BF16)
        stage_b(zcur, znext, ucur, hprev)

        @pl.when(s == last)
        def _():
            wait_tile(nbuf, 1 - q)

    @pl.when(lax.rem(s, 2) == 0)
    def _():
        fused_step(0, z1, z0, h0, h1, u0, ybuf0, ybuf1)

    @pl.when(lax.rem(s, 2) == 1)
    def _():
        fused_step(1, z0, z1, h1, h0, u1, ybuf1, ybuf0)


def _ple_update(hmid_ref, route_ref, p_ref, g3_ref, wple_ref, wpg_ref, buf, tp, slabs):
    r = route_ref[...]
    y0 = _unpack_pairs_f32(_load_row_slabs(buf, 0, tp, slabs))
    y1 = _unpack_pairs_f32(_load_row_slabs(buf, tp, tp, slabs))
    h2 = hmid_ref[...] + r[:, 2:3] * y0 + r[:, 3:4] * y1
    u3 = _rms_norm(h2, g3_ref[...]).astype(BF16)
    gate = _sigmoid(_dot(u3, wpg_ref[...]))
    return h2 + gate * _dot(p_ref[...].astype(BF16), wple_ref[...])


def _rank_kernel(route_ref, counts_ref, pos_ref, tri_scr, run_scr, start_scr, *, tr, n_experts):
    i = pl.program_id(0)

    @pl.when(i == 0)
    def _():
        r_i = lax.broadcasted_iota(I32, (tr, tr), 0)
        c_i = lax.broadcasted_iota(I32, (tr, tr), 1)
        tri_scr[...] = jnp.where(r_i > c_i, 1.0, 0.0).astype(BF16)
        run_scr[...] = jnp.zeros_like(run_scr)
        cnt = counts_ref[...].astype(I32)
        padded = ((cnt + (EXPERT_BLOCK - 1)) // EXPERT_BLOCK) * EXPERT_BLOCK
        lane8 = lax.broadcasted_iota(I32, (SUBLANES, LANES), 1)
        incl = padded
        k = 1
        while k < n_experts:
            incl = incl + jnp.where(lane8 >= k, pltpu.roll(incl, k, 1), 0)
            k *= 2
        start_scr[...] = (incl - padded).astype(F32)

    r = route_ref[...]
    lane = lax.broadcasted_iota(I32, (tr, LANES), 1).astype(F32)
    oh0 = lane == r[:, 0:1]
    oh1 = lane == r[:, 1:2]
    oh = jnp.where(oh0 | oh1, 1.0, 0.0)
    before = _dot(tri_scr[...], oh.astype(BF16)) + (run_scr[0:1, :] + start_scr[0:1, :])
    rk0 = jnp.sum(jnp.where(oh0, before, 0.0), axis=-1, keepdims=True)
    rk1 = jnp.sum(jnp.where(oh1, before, 0.0), axis=-1, keepdims=True)
    cols = jnp.where(lane == 0, rk0, jnp.where(lane == 1, rk1, 0.0))
    pos_ref[...] = jnp.transpose(cols)[0:SUBLANES, :].astype(I32)
    run_scr[...] = run_scr[...] + jnp.sum(oh, axis=0, keepdims=True)


def _dispatch_kernel(cnt_ref, end_ref, pos_ref, u2_ref, xs_ref, zero_scr, sem, zsem, *, td, slabs, n_experts):
    @pl.when(pl.program_id(0) == 0)
    def _():
        zero_scr[...] = jnp.zeros_like(zero_scr)
        n_blocks = xs_ref.shape[0] // (EXPERT_BLOCK * slabs)
        n_used = end_ref[n_experts - 1] // EXPERT_BLOCK

        def zero_block(blk):
            row0 = pl.multiple_of(blk * (EXPERT_BLOCK * slabs), EXPERT_BLOCK * slabs)
            return pltpu.make_async_copy(zero_scr, xs_ref.at[pl.ds(row0, EXPERT_BLOCK * slabs)], zsem)

        todo = [(cnt_ref[e] > 0, end_ref[e] // EXPERT_BLOCK - 1) for e in range(n_experts)]
        todo += [(blk >= n_used, blk) for blk in range(max(n_blocks - n_experts - EXPERT_PAIR + 1, 0), n_blocks)]
        for cond, blk in todo:
            @pl.when(cond)
            def _():
                zero_block(blk).start()
        for cond, blk in todo:
            @pl.when(cond)
            def _():
                zero_block(blk).wait()

    def issue(q, carry):
        for uu in range(DMA_UNROLL):
            t = q * DMA_UNROLL + uu
            for k in range(TOP_K):
                dst = pl.multiple_of(pos_ref[k, t] * slabs, slabs)
                pltpu.make_async_copy(u2_ref.at[pl.ds(t * slabs, slabs)], xs_ref.at[pl.ds(dst, slabs)],
                                      sem).start(priority=k)
        return carry

    lax.fori_loop(0, td // DMA_UNROLL, issue, 0)
    for _ in range(TOP_K):
        pltpu.make_async_copy(u2_ref, xs_ref.at[pl.ds(0, td * slabs)], sem).wait()


def _expert_kernel(be_ref, nu_ref, xs_ref, *rest, slabs):
    w_f32 = (rest[0:3], rest[3:6])
    ys_ref = rest[6]
    w_bf16 = (rest[7:10], rest[10:13])
    i = pl.program_id(0)
    half = slabs * LANES
    blk_rows = EXPERT_BLOCK * slabs

    for b in range(EXPERT_PAIR):
        j = EXPERT_PAIR * i + b

        @pl.when((i == 0) | (be_ref[j] != be_ref[jnp.maximum(j - EXPERT_PAIR, 0)]))
        def _():
            for dst, src in zip(w_bf16[b], w_f32[b]):
                dst[...] = src[...].astype(BF16)

    def ffn(b):
        wg_b, wu_b, wd_b = w_bf16[b]
        lo, hi = _unpack_bf16_pairs(_load_row_slabs(xs_ref, b * EXPERT_BLOCK, EXPERT_BLOCK, slabs))
        gate = _dot(lo, wg_b[0:half, :]) + _dot(hi, wg_b[half:, :])
        up = _dot(lo, wu_b[0:half, :]) + _dot(hi, wu_b[half:, :])
        act = (gate * _sigmoid(gate)) * up
        _store_row_slabs(ys_ref, _pack_bf16_pairs(_dot(act.astype(BF16), wd_b[...])), b * EXPERT_BLOCK)

    def zero(b):
        ys_ref[pl.ds(b * blk_rows, blk_rows), :] = jnp.zeros((blk_rows, LANES), U32)

    n_valid = nu_ref[0] - EXPERT_PAIR * i

    @pl.when(n_valid >= 2)
    def _():
        ffn(0)
        ffn(1)

    @pl.when(n_valid == 1)
    def _():
        ffn(0)
        zero(1)

    @pl.when(n_valid <= 0)
    def _():
        zero(0)
        zero(1)


def _combine_kernel(pos_cur_ref, pos_nxt_ref, hmid_ref, route_ref, p_ref, g3_ref, wple_ref, wpg_ref, gf_ref,
                    ys_ref, out_ref, ybuf0, ybuf1, sems, *, tp, slabs):
    i = pl.program_id(0)
    n = pl.num_programs(0)

    def gather_row(pos_ref, buf, s, t, k):
        src = pl.multiple_of(pos_ref[k, t] * slabs, slabs)
        return pltpu.make_async_copy(ys_ref.at[pl.ds(src, slabs)], buf.at[pl.ds((k * tp + t) * slabs, slabs)],
                                     sems.at[s])

    def wait_tile(buf, s):
        for k in range(TOP_K):
            pltpu.make_async_copy(ys_ref.at[pl.ds(0, tp * slabs)], buf.at[pl.ds(k * tp * slabs, tp * slabs)],
                                  sems.at[s]).wait()

    @pl.when(i == 0)
    def _():
        def body(q, carry):
            for uu in range(DMA_UNROLL):
                for k in range(TOP_K):
                    gather_row(pos_cur_ref, ybuf0, 0, q * DMA_UNROLL + uu, k).start(priority=k)
            return carry
        lax.fori_loop(0, tp // DMA_UNROLL, body, 0)

    def step(buf, s, nbuf, ns):
        wait_tile(buf, s)
        for t in range(tp):
            for k in range(TOP_K):
                gather_row(pos_nxt_ref, nbuf, ns, t, k).start(priority=k)
        h3 = _ple_update(hmid_ref, route_ref, p_ref, g3_ref, wple_ref, wpg_ref, buf, tp, slabs)
        out_ref[...] = _rms_norm(h3, gf_ref[...])

        @pl.when(i == n - 1)
        def _():
            wait_tile(nbuf, ns)

    @pl.when(lax.rem(i, 2) == 0)
    def _():
        step(ybuf0, 0, ybuf1, 1)

    @pl.when(lax.rem(i, 2) == 1)
    def _():
        step(ybuf1, 1, ybuf0, 0)


def _const_spec(shape, layer):
    nd = len(shape)
    return pl.BlockSpec((None,) + tuple(shape), lambda i: (layer,) + (0,) * nd, pipeline_mode=pl.Buffered(1))


def _block_diag_gates(w_r, w_i):
    n_layers, n_heads, hd, _ = w_r.shape
    per = LANES // hd
    n_slab = n_heads // per

    def bd(w):
        w = w.reshape(n_layers, n_slab, per, hd, hd)
        eye = jnp.eye(per, dtype=w.dtype)
        out = jnp.einsum('lspij,pq->lspiqj', w, eye)
        return out.reshape(n_layers, n_slab, LANES, LANES)

    return jnp.concatenate([bd(w_r), bd(w_i)], axis=-1).astype(BF16)


def _tile_sizes(seq, n_tok):
    mixer_rows = min(256, seq * SEQ_GROUP)
    rank_rows = min(2048, n_tok)
    combine_rows = min(256, n_tok)
    assert (seq * SEQ_GROUP) % mixer_rows == 0 and n_tok % rank_rows == 0 and rank_rows % combine_rows == 0
    assert mixer_rows % SEQ_GROUP == 0 and rank_rows % DMA_UNROLL == 0 and combine_rows % DMA_UNROLL == 0
    return mixer_rows, rank_rows, combine_rows


def kernel(x, p, norm_mix_g, w_in, conv_w, conv_b, w_rg, b_rg, w_ig, b_ig, lru_lambda, w_pool, pool_scale, w_up_a, w_up_b, w_out, norm_ffn_g, w_router_group, b_router_group, w_router_expert, b_router_expert, w_gate_e, w_up_e, w_down_e, norm_ple_g, w_ple, w_ple_gate, final_norm_g):
    bsz, seq, d_model = x.shape
    n_layers, _, d_in = w_in.shape
    d_lru = conv_w.shape[2]
    d_pool = pool_scale.shape[1]
    n_groups = w_router_group.shape[2]
    n_experts = w_router_expert.shape[2]
    experts_per_group = n_experts // n_groups
    d_expert = w_gate_e.shape[3]
    ple_dim = w_ple.shape[1]
    n_tok = bsz * seq
    half = d_model // 2
    x_slabs = half // LANES

    rows_m, tr, tp = _tile_sizes(seq, n_tok)
    assert tp == rows_m
    assert bsz % SEQ_GROUP == 0
    assert d_in == 2 * d_lru + d_pool + 2 * d_model
    assert LANES % w_rg.shape[2] == 0 and d_lru % LANES == 0 and d_model % (2 * LANES) == 0
    assert w_pool.shape[2] == LANES and (2 << (w_pool.shape[1] - 1)) <= HALO_STEPS and conv_w.shape[1] - 1 <= HALO_STEPS
    assert n_groups + n_experts <= LANES

    n_blocks = (n_tok * TOP_K + n_experts * (EXPERT_BLOCK - 1) + EXPERT_BLOCK - 1) // EXPERT_BLOCK
    n_blocks = -(-n_blocks // EXPERT_PAIR) * EXPERT_PAIR
    n_rows = n_blocks * EXPERT_BLOCK

    w_in_b = w_in.astype(BF16)
    w_gate_bd = _block_diag_gates(w_rg, w_ig)
    w_pool_b = w_pool.astype(BF16)
    w_up_a_b = w_up_a.astype(BF16)
    w_up_b_b = w_up_b.astype(BF16)
    w_out_b = w_out.astype(BF16)
    w_router = jnp.concatenate([w_router_group, w_router_expert], axis=-1)
    w_router = jnp.pad(w_router, ((0, 0), (0, 0), (0, LANES - w_router.shape[-1]))).astype(BF16)
    b_router = jnp.concatenate([b_router_group, b_router_expert], axis=-1)
    b_router = jnp.pad(b_router, ((0, 0), (0, LANES - b_router.shape[-1])))[:, None, :]
    w_ple_b = w_ple.astype(BF16)
    w_pg_b = w_ple_gate.astype(BF16)
    row3 = lambda a: a[:, None, :]
    g1, g2, g3 = row3(norm_mix_g), row3(norm_ffn_g), row3(norm_ple_g)
    conv_b3, b_rg3, b_ig3, lam3, pscale3 = row3(conv_b), row3(b_rg), row3(b_ig), row3(lru_lambda), row3(pool_scale)
    gf = final_norm_g[None, :]

    n_sg = bsz // SEQ_GROUP
    h = x.reshape(n_sg, SEQ_GROUP, seq, d_model).transpose(0, 2, 1, 3).reshape(n_tok, d_model)
    p2 = p.reshape(n_layers, n_sg, SEQ_GROUP, seq, ple_dim).transpose(0, 1, 3, 2, 4).reshape(n_layers, n_tok, ple_dim)
    n_chunks = seq * SEQ_GROUP // rows_m
    n_mt = n_tok // rows_m
    n_rt = n_tok // tr
    per_rt = tr // tp
    n_ct = n_tok // tp

    def pos_spec(shift, n_tiles):
        return pl.BlockSpec(
            (None, SUBLANES, tp),
            lambda i: (jnp.minimum(i + shift, n_tiles - 1) // per_rt, 0, jnp.minimum(i + shift, n_tiles - 1) % per_rt),
            memory_space=pltpu.SMEM)

    for layer in range(n_layers):
        cs = functools.partial(_const_spec, layer=layer)
        in_map = lambda s: (jnp.minimum(s, n_mt - 1), 0)
        out_map = lambda s: (jnp.maximum(s - 1, 0), 0)
        fused = layer > 0
        if fused:
            csp = functools.partial(_const_spec, layer=layer - 1)
            head_specs = [
                pos_spec(0, n_mt), pos_spec(1, n_mt),
                pl.BlockSpec((rows_m, d_model), in_map),
                pl.BlockSpec((rows_m, LANES), in_map),
                pl.BlockSpec((None, rows_m, ple_dim), lambda s, layer=layer: (layer - 1, jnp.minimum(s, n_mt - 1), 0)),
                csp((1, d_model)), csp((ple_dim, d_model)), csp((d_model, d_model)), cs((1, d_model)),
                pl.BlockSpec(memory_space=pl.ANY),
            ]
            head_args = (pos, pos, h_mid, route, p2, g3, w_ple_b, w_pg_b, g1, ys)
            head_scratch = [
                pltpu.VMEM((rows_m, d_model), F32), pltpu.VMEM((rows_m, d_model), F32),
                pltpu.VMEM((rows_m, d_model), BF16), pltpu.VMEM((rows_m, d_model), BF16),
                pltpu.VMEM((TOP_K * rows_m * x_slabs, LANES), U32), pltpu.VMEM((TOP_K * rows_m * x_slabs, LANES), U32),
                pltpu.SemaphoreType.DMA((2,)),
            ]
        else:
            head_specs = [pl.BlockSpec((rows_m, d_model), in_map), pl.BlockSpec((rows_m, d_model), out_map),
                          cs((1, d_model))]
            head_args = (h, h, g1)
            head_scratch = [pltpu.VMEM((rows_m, d_model), BF16), pltpu.VMEM((rows_m, d_model), BF16)]
        h_mid, u2, route, counts = pl.pallas_call(
            functools.partial(_mixer_kernel, fused=fused, rows=rows_m, n_chunks=n_chunks, d_model=d_model,
                              d_lru=d_lru, d_pool=d_pool, n_groups=n_groups, experts_per_group=experts_per_group,
                              slabs=x_slabs),
            grid=(n_mt + 1,),
            in_specs=head_specs + [
                cs((d_model, d_in)), cs(conv_w.shape[1:]), cs((1, d_lru)),
                cs(w_gate_bd.shape[1:]), cs((1, d_lru)), cs((1, d_lru)), cs((1, d_lru)),
                cs(w_pool.shape[1:]), cs((1, d_pool)), cs((d_lru, d_model)), cs((d_pool, d_model)),
                cs((d_model, d_model)), cs((1, d_model)), cs((d_model, LANES)), cs((1, LANES)),
            ],
            out_specs=[
                pl.BlockSpec((rows_m, d_model), out_map),
                pl.BlockSpec((rows_m * x_slabs, LANES), out_map),
                pl.BlockSpec((rows_m, LANES), out_map),
                pl.BlockSpec((SUBLANES, LANES), lambda s: (0, 0)),
            ],
            out_shape=[
                jax.ShapeDtypeStruct((n_tok, d_model), F32),
                jax.ShapeDtypeStruct((n_tok * x_slabs, LANES), U32),
                jax.ShapeDtypeStruct((n_tok, LANES), F32),
                jax.ShapeDtypeStruct((SUBLANES, LANES), F32),
            ],
            scratch_shapes=[
                pltpu.VMEM((rows_m, d_in), F32), pltpu.VMEM((rows_m, d_in), F32),
                pltpu.VMEM((HALO + rows_m, d_lru), F32),
                pltpu.VMEM((HALO + rows_m, d_pool), F32),
                pltpu.VMEM((SEQ_GROUP, d_lru), F32),
                pltpu.VMEM((rows_m, d_lru), BF16),
                pltpu.VMEM((rows_m, d_pool), BF16),
                pltpu.VMEM((SUBLANES, LANES), F32),
            ] + head_scratch,
            compiler_params=pltpu.CompilerParams(dimension_semantics=("arbitrary",), vmem_limit_bytes=58 * MIB),
            name=f"mixer_l{layer}",
        )(*head_args, w_in_b, conv_w, conv_b3, w_gate_bd, b_rg3, b_ig3, lam3, w_pool_b, pscale3,
          w_up_a_b, w_up_b_b, w_out_b, g2, w_router, b_router)

        pos = pl.pallas_call(
            functools.partial(_rank_kernel, tr=tr, n_experts=n_experts),
            grid=(n_rt,),
            in_specs=[pl.BlockSpec((tr, LANES), lambda i: (i, 0)),
                      pl.BlockSpec((SUBLANES, LANES), lambda i: (0, 0))],
            out_specs=pl.BlockSpec((None, SUBLANES, tr), lambda i: (i, 0, 0)),
            out_shape=jax.ShapeDtypeStruct((n_rt, SUBLANES, tr), I32),
            scratch_shapes=[pltpu.VMEM((tr, tr), BF16), pltpu.VMEM((SUBLANES, LANES), F32),
                            pltpu.VMEM((SUBLANES, LANES), F32)],
            compiler_params=pltpu.CompilerParams(dimension_semantics=("arbitrary",)),
            name=f"rank_l{layer}",
        )(route, counts)

        cnt = counts[0, :n_experts].astype(I32)
        pad_end = jnp.cumsum(((cnt + EXPERT_BLOCK - 1) // EXPERT_BLOCK) * EXPERT_BLOCK)
        n_used = (pad_end[-1] // EXPERT_BLOCK).astype(I32)
        block_row0 = jnp.minimum(jnp.arange(n_blocks, dtype=I32), n_used - 1) * EXPERT_BLOCK
        block_expert = jnp.minimum(jnp.sum(pad_end[None, :] <= block_row0[:, None], axis=1), n_experts - 1).astype(I32)

        xs = pl.pallas_call(
            functools.partial(_dispatch_kernel, td=tr, slabs=x_slabs, n_experts=n_experts),
            grid_spec=pltpu.PrefetchScalarGridSpec(
                num_scalar_prefetch=2,
                grid=(n_rt,),
                in_specs=[
                    pl.BlockSpec((None, SUBLANES, tr), lambda i, cnt, end: (i, 0, 0), memory_space=pltpu.SMEM),
                    pl.BlockSpec((tr * x_slabs, LANES), lambda i, cnt, end: (i, 0)),
                ],
                out_specs=pl.BlockSpec(memory_space=pl.ANY),
                scratch_shapes=[pltpu.VMEM((EXPERT_BLOCK * x_slabs, LANES), U32),
                                pltpu.SemaphoreType.DMA, pltpu.SemaphoreType.DMA],
            ),
            out_shape=jax.ShapeDtypeStruct((n_rows * x_slabs, LANES), U32),
            compiler_params=pltpu.CompilerParams(dimension_semantics=("arbitrary",), has_side_effects=True),
            name=f"dispatch_l{layer}",
        )(cnt, pad_end, pos, u2)

        pair_rows = EXPERT_PAIR * EXPERT_BLOCK * x_slabs
        used_map = lambda i, be, nu: (jnp.minimum(i, (nu[0] - 1) // EXPERT_PAIR), 0)
        w_specs, w_scratch = [], []
        for b in range(EXPERT_PAIR):
            w_map = lambda i, be, nu, layer=layer, b=b: (layer, be[EXPERT_PAIR * i + b], 0, 0)
            for shape in ((d_model, d_expert), (d_model, d_expert), (d_expert, d_model)):
                w_specs.append(pl.BlockSpec((None, None) + shape, w_map))
                w_scratch.append(pltpu.VMEM(shape, BF16))
        ys = pl.pallas_call(
            functools.partial(_expert_kernel, slabs=x_slabs),
            grid_spec=pltpu.PrefetchScalarGridSpec(
                num_scalar_prefetch=2,
                grid=(n_blocks // EXPERT_PAIR,),
                in_specs=[pl.BlockSpec((pair_rows, LANES), used_map)] + w_specs,
                out_specs=pl.BlockSpec((pair_rows, LANES), lambda i, be, nu: (i, 0)),
                scratch_shapes=w_scratch,
            ),
            out_shape=jax.ShapeDtypeStruct((n_rows * x_slabs, LANES), U32),
            compiler_params=pltpu.CompilerParams(dimension_semantics=("arbitrary",), vmem_limit_bytes=48 * MIB),
            name=f"experts_l{layer}",
        )(block_expert, n_used[None], xs, *([w_gate_e, w_up_e, w_down_e] * EXPERT_PAIR))

    last = n_layers - 1
    csl = functools.partial(_const_spec, layer=last)
    tok_spec = pl.BlockSpec((tp, d_model), lambda i: (i, 0))
    h = pl.pallas_call(
        functools.partial(_combine_kernel, tp=tp, slabs=x_slabs),
        grid=(n_ct,),
        in_specs=[
            pos_spec(0, n_ct), pos_spec(1, n_ct),
            tok_spec,
            pl.BlockSpec((tp, LANES), lambda i: (i, 0)),
            pl.BlockSpec((None, tp, ple_dim), lambda i: (last, i, 0)),
            csl((1, d_model)), csl((ple_dim, d_model)), csl((d_model, d_model)),
            pl.BlockSpec((1, d_model), lambda i: (0, 0)),
            pl.BlockSpec(memory_space=pl.ANY),
        ],
        out_specs=tok_spec,
        out_shape=jax.ShapeDtypeStruct((n_tok, d_model), F32),
        scratch_shapes=[pltpu.VMEM((TOP_K * tp * x_slabs, LANES), U32),
                        pltpu.VMEM((TOP_K * tp * x_slabs, LANES), U32),
                        pltpu.SemaphoreType.DMA((2,))],
        compiler_params=pltpu.CompilerParams(dimension_semantics=("arbitrary",)),
        name="combine_final",
    )(pos, pos, h_mid, route, p2, g3, w_ple_b, w_pg_b, gf, ys)

    out = h.reshape(n_sg, seq, SEQ_GROUP, d_model).transpose(0, 2, 1, 3)
    return out.reshape(bsz, seq, d_model)
```

```python
import functools

import jax
import jax.numpy as jnp
from jax import lax
from jax.experimental import pallas as pl
from jax.experimental.pallas import tpu as pltpu

EPS = 1e-6
LRU_C = 8.0
EXPERT_BLOCK = 256
EXPERT_PAIR = 2
TOP_K = 2
GELU_C0 = 0.7978845608028654
GELU_C1 = 0.044715

LANES = 128
SUBLANES = 8
SEQ_GROUP = SUBLANES
HALO_STEPS = 16
HALO = HALO_STEPS * SEQ_GROUP
DMA_UNROLL = 8

F32 = jnp.float32
BF16 = jnp.bfloat16
U32 = jnp.uint32
I32 = jnp.int32
MIB = 1024 * 1024


def _dot(a, b):
    return jnp.dot(a, b, preferred_element_type=F32)


def _sigmoid(x):
    return 1.0 / (1.0 + jnp.exp(-x))


def _rms_norm(x, g):
    return x * lax.rsqrt(jnp.mean(x * x, axis=-1, keepdims=True) + EPS) * g


def _gelu_tanh(x):
    return 0.5 * x * (1.0 + jnp.tanh(GELU_C0 * (x + GELU_C1 * (x * x * x))))


def _softplus(x):
    return jnp.maximum(x, 0.0) + jnp.log1p(jnp.exp(-jnp.abs(x)))


def _sqrt_nonneg(x):
    return jnp.where(x > 0.0, x * lax.rsqrt(x), 0.0)


def _pack_bf16_pairs(x):
    half = x.shape[1] // 2
    bits = lax.bitcast_convert_type(x.astype(BF16).astype(F32), U32)
    return (bits[:, :half] >> 16) | (bits[:, half:] & jnp.uint32(0xFFFF0000))


def _unpack_bf16_pairs(w):
    lo = lax.bitcast_convert_type(w << 16, F32).astype(BF16)
    hi = lax.bitcast_convert_type(w & jnp.uint32(0xFFFF0000), F32).astype(BF16)
    return lo, hi


def _unpack_pairs_f32(w):
    return jnp.concatenate([lax.bitcast_convert_type(w << 16, F32),
                            lax.bitcast_convert_type(w & jnp.uint32(0xFFFF0000), F32)], axis=1)


def _store_row_slabs(ref, val, row0=0):
    n = val.shape[1] // LANES
    for c in range(n):
        ref[pl.ds(row0 * n + c, val.shape[0], stride=n), :] = val[:, c * LANES:(c + 1) * LANES]


def _load_row_slabs(ref, row0, n_rows, n):
    return jnp.concatenate([ref[pl.ds(row0 * n + c, n_rows, stride=n), :] for c in range(n)], axis=1)


class _SlabTile:
    def __init__(self, ref):
        self.ref = ref

    def __getitem__(self, idx):
        return jnp.concatenate([self.ref[cc] for cc in range(self.ref.shape[0])], axis=1)


def _mixer_kernel(*refs, fused, rows, n_chunks, d_model, d_lru, d_pool, n_groups, experts_per_group, slabs):
    if fused:
        (pos_cur_ref, pos_nxt_ref, hprev_ref, rprev_ref, p_ref, g3_ref, wple_ref, wpg_ref, g1_ref, ys_ref), refs = refs[:10], refs[10:]
    else:
        (x_ref, g1_ref), refs = refs[:2], refs[2:]
    (win_ref, convw_ref, convb_ref, wgate_ref, brg_ref, big_ref, lam_ref, wpool_ref, pscale_ref, wupa_ref,
     wupb_ref, wout_ref, g2_ref, wr_ref, br_ref, hmid_ref, u2_ref, route_ref, counts_ref,
     z0, z1, xa_ext, xb_ext, hstate, ha_scr, hb_scr, cnt_scr), refs = refs[:27], refs[27:]
    if fused:
        h0, h1, u0, u1, ybuf0, ybuf1, sems = refs
    else:
        h0, h1, u0, u1 = refs
    s = pl.program_id(0)
    steps = rows // SEQ_GROUP
    o_pool = 2 * d_lru
    o_gate = o_pool + d_pool
    c = lax.rem(jnp.maximum(s - 1, 0), n_chunks)

    @pl.when(s == 0)
    def _():
        z1[...] = jnp.zeros_like(z1)
        h1[...] = jnp.zeros_like(h1)

    @pl.when(s <= 1)
    def _():
        cnt_scr[...] = jnp.zeros_like(cnt_scr)

    @pl.when(c == 0)
    def _():
        xa_ext[0:HALO, :] = jnp.zeros((HALO, d_lru), F32)
        xb_ext[0:HALO, :] = jnp.zeros((HALO, d_pool), F32)
        hstate[...] = jnp.zeros_like(hstate)

    a_cols = 2 * LANES
    n_hooks = d_lru // LANES + 2
    skip_hooks = 1

    def stage_b(zbuf, znext, u_ref, h_ref):
        pending = list(range(0, win_ref.shape[1], a_cols))
        hooks_left = [n_hooks]

        def stage_a_chunks():
            hooks_left[0] -= 1
            if hooks_left[0] >= n_hooks - skip_hooks:
                return
            n = -(-len(pending) // (hooks_left[0] + 1))
            for _ in range(n):
                lo = pending.pop(0)
                znext[:, lo:lo + a_cols] = _dot(u_ref[...], win_ref[:, lo:lo + a_cols])

        conv_k = convw_ref.shape[0]
        xa_pre = zbuf[:, 0:d_lru]
        xa_ext[HALO:HALO + rows, :] = xa_pre
        xa = convb_ref[...] + convw_ref[conv_k - 1:conv_k, :] * xa_pre
        for j in range(1, conv_k):
            xa = xa + convw_ref[conv_k - 1 - j:conv_k - j, :] * xa_ext[HALO - j * SEQ_GROUP:HALO - j * SEQ_GROUP + rows, :]
        xa_ext[0:HALO, :] = xa_ext[rows:rows + HALO, :]

        for p in range(d_lru // LANES):
            stage_a_chunks()
            sl = slice(p * LANES, (p + 1) * LANES)
            xa_s = xa[:, sl]
            rg = _dot(xa_s.astype(BF16), wgate_ref[p])
            r = _sigmoid(rg[:, :LANES] + brg_ref[:, sl])
            i = _sigmoid(rg[:, LANES:] + big_ref[:, sl])
            log_a = r * (-LRU_C * _softplus(-lam_ref[:, sl]))
            a = jnp.exp(log_a)
            b = _sqrt_nonneg(1.0 - a * a) * (i * xa_s)
            h = hstate[:, sl]
            hs = []
            for t in range(steps):
                rs = slice(t * SEQ_GROUP, (t + 1) * SEQ_GROUP)
                h = a[rs] * h + b[rs]
                hs.append(h)
            hstate[:, sl] = h
            hseq = jnp.concatenate(hs, axis=0)
            ha_scr[:, sl] = (hseq * _gelu_tanh(zbuf[:, d_lru + p * LANES:d_lru + (p + 1) * LANES])).astype(BF16)

        stage_a_chunks()
        xb_ext[HALO:HALO + rows, :] = zbuf[:, o_pool:o_gate]
        t_glob = (c * steps + lax.broadcasted_iota(I32, (rows, LANES), 0) // SEQ_GROUP).astype(F32)
        for g in range(d_pool // LANES):
            sl = slice(g * LANES, (g + 1) * LANES)
            window = 2 << g
            acc = xb_ext[:, sl]
            span = 1
            while span < window:
                acc = acc[span * SEQ_GROUP:] + acc[:-span * SEQ_GROUP]
                span *= 2
            acc = acc[acc.shape[0] - rows:]
            xb_s = xb_ext[HALO:HALO + rows, sl]
            pooled = acc / jnp.minimum(t_glob + 1.0, float(window)) - xb_s
            mixed = _dot(pooled.astype(BF16), wpool_ref[g]) * pscale_ref[:, sl]
            hb_scr[:, sl] = mixed.astype(BF16)
        xb_ext[0:HALO, :] = xb_ext[rows:rows + HALO, :]

        stage_a_chunks()
        assert not pending
        merged = (_sigmoid(zbuf[:, o_gate:o_gate + d_model]) * _dot(ha_scr[...], wupa_ref[...])
                  + _sigmoid(zbuf[:, o_gate + d_model:o_gate + 2 * d_model]) * _dot(hb_scr[...], wupb_ref[...]))
        h_new = h_ref[...] + _dot(merged.astype(BF16), wout_ref[...])
        hmid_ref[...] = h_new

        u2 = _rms_norm(h_new, g2_ref[...])
        _store_row_slabs(u2_ref, _pack_bf16_pairs(u2))
        logits = _dot(u2.astype(BF16), wr_ref[...]) + br_ref[...]
        lane = lax.broadcasted_iota(I32, (rows, LANES), 1).astype(F32)
        neg = -jnp.inf
        big = float(LANES)
        is_group = lane < n_groups
        glog = jnp.where(is_group, logits, neg)
        gmax = jnp.max(glog, axis=-1, keepdims=True)
        gidx = jnp.min(jnp.where(glog == gmax, lane, big), axis=-1, keepdims=True)
        psum = jnp.sum(jnp.where(is_group, jnp.exp(glog - gmax), 0.0), axis=-1, keepdims=True)
        lo = n_groups + gidx * experts_per_group
        in_group = (lane >= lo) & (lane < lo + experts_per_group)
        sel = jnp.where(in_group, logits, neg)
        m1 = jnp.max(sel, axis=-1, keepdims=True)
        i1 = jnp.min(jnp.where(sel == m1, lane, big), axis=-1, keepdims=True)
        sel2 = jnp.where(lane == i1, neg, sel)
        m2 = jnp.max(sel2, axis=-1, keepdims=True)
        i2 = jnp.min(jnp.where(sel2 == m2, lane, big), axis=-1, keepdims=True)
        e21 = jnp.exp(m2 - m1)
        w1 = 1.0 / (psum * (1.0 + e21))
        w2 = e21 * w1
        e1 = i1 - n_groups
        e2 = i2 - n_groups
        route_ref[...] = jnp.where(lane == 0, e1, jnp.where(lane == 1, e2,
                                   jnp.where(lane == 2, w1, jnp.where(lane == 3, w2, 0.0))))
        chosen = jnp.where((lane == e1) | (lane == e2), 1.0, 0.0)
        cnt_scr[...] = cnt_scr[...] + jnp.sum(chosen, axis=0, keepdims=True)
        counts_ref[...] = cnt_scr[...]

    if not fused:
        def first_layer_step(zcur, znext, hcur, hprev, ucur):
            for b in range(SEQ_GROUP):
                for cc in range(d_model // LANES):
                    hcur.at[cc][pl.ds(b, steps, stride=SEQ_GROUP), :] = x_ref[b, :, cc * LANES:(cc + 1) * LANES]
            ucur[...] = _rms_norm(_SlabTile(hcur)[...], g1_ref[...]).astype(BF16)
            stage_b(zcur, znext, ucur, _SlabTile(hprev))

        @pl.when(lax.rem(s, 2) == 0)
        def _():
            first_layer_step(z1, z0, h0, h1, u0)

        @pl.when(lax.rem(s, 2) == 1)
        def _():
            first_layer_step(z0, z1, h1, h0, u1)
        return

    last = pl.num_programs(0) - 1

    def gather_row(pos_ref, buf, q, t, k):
        src = pl.multiple_of(pos_ref[k, t] * slabs, slabs)
        return pltpu.make_async_copy(ys_ref.at[pl.ds(src, slabs)], buf.at[pl.ds((k * rows + t) * slabs, slabs)],
                                     sems.at[q])

    def wait_tile(buf, q):
        for k in range(TOP_K):
            pltpu.make_async_copy(ys_ref.at[pl.ds(0, rows * slabs)], buf.at[pl.ds(k * rows * slabs, rows * slabs)],
                                  sems.at[q]).wait()

    @pl.when(s == 0)
    def _():
        def body(i, carry):
            for uu in range(DMA_UNROLL):
                for k in range(TOP_K):
                    gather_row(pos_cur_ref, ybuf0, 0, i * DMA_UNROLL + uu, k).start(priority=k)
            return carry
        lax.fori_loop(0, rows // DMA_UNROLL, body, 0)

    def fused_step(q, zcur, znext, hcur, hprev, ucur, buf, nbuf):
        wait_tile(buf, q)
        for t in range(rows):
            for k in range(TOP_K):
                gather_row(pos_nxt_ref, nbuf, 1 - q, t, k).start(priority=k)
        h3 = _ple_update(hprev_ref, rprev_ref, p_ref, g3_ref, wple_ref, wpg_ref, buf, rows, slabs)
        hcur[...] = h3
        ucur[...] = _rms_norm(h3, g1_ref[...]).astype(BF16)
        stage_b(zcur, znext, ucur, hprev)

        @pl.when(s == last)
        def _():
            wait_tile(nbuf, 1 - q)

    @pl.when(lax.rem(s, 2) == 0)
    def _():
        fused_step(0, z1, z0, h0, h1, u0, ybuf0, ybuf1)

    @pl.when(lax.rem(s, 2) == 1)
    def _():
        fused_step(1, z0, z1, h1, h0, u1, ybuf1, ybuf0)


def _ple_update(hmid_ref, route_ref, p_ref, g3_ref, wple_ref, wpg_ref, buf, tp, slabs):
    r = route_ref[...]
    y0 = _unpack_pairs_f32(_load_row_slabs(buf, 0, tp, slabs))
    y1 = _unpack_pairs_f32(_load_row_slabs(buf, tp, tp, slabs))
    h2 = hmid_ref[...] + r[:, 2:3] * y0 + r[:, 3:4] * y1
    u3 = _rms_norm(h2, g3_ref[...]).astype(BF16)
    gate = _sigmoid(_dot(u3, wpg_ref[...]))
    return h2 + gate * _dot(p_ref[...].astype(BF16), wple_ref[...])


def _rank_kernel(route_ref, counts_ref, pos_ref, tri_scr, run_scr, start_scr, *, tr, n_experts):
    i = pl.program_id(0)

    @pl.when(i == 0)
    def _():
        r_i = lax.broadcasted_iota(I32, (tr, tr), 0)
        c_i = lax.broadcasted_iota(I32, (tr, tr), 1)
        tri_scr[...] = jnp.where(r_i > c_i, 1.0, 0.0).astype(BF16)
        run_scr[...] = jnp.zeros_like(run_scr)
        cnt = counts_ref[...].astype(I32)
        padded = ((cnt + (EXPERT_BLOCK - 1)) // EXPERT_BLOCK) * EXPERT_BLOCK
        lane8 = lax.broadcasted_iota(I32, (SUBLANES, LANES), 1)
        incl = padded
        k = 1
        while k < n_experts:
            incl = incl + jnp.where(lane8 >= k, pltpu.roll(incl, k, 1), 0)
            k *= 2
        start_scr[...] = (incl - padded).astype(F32)

    r = route_ref[...]
    lane = lax.broadcasted_iota(I32, (tr, LANES), 1).astype(F32)
    oh0 = lane == r[:, 0:1]
    oh1 = lane == r[:, 1:2]
    oh = jnp.where(oh0 | oh1, 1.0, 0.0)
    before = _dot(tri_scr[...], oh.astype(BF16)) + (run_scr[0:1, :] + start_scr[0:1, :])
    rk0 = jnp.sum(jnp.where(oh0, before, 0.0), axis=-1, keepdims=True)
    rk1 = jnp.sum(jnp.where(oh1, before, 0.0), axis=-1, keepdims=True)
    cols = jnp.where(lane == 0, rk0, jnp.where(lane == 1, rk1, 0.0))
    pos_ref[...] = jnp.transpose(cols)[0:SUBLANES, :].astype(I32)
    run_scr[...] = run_scr[...] + jnp.sum(oh, axis=0, keepdims=True)


def _dispatch_kernel(cnt_ref, end_ref, pos_ref, u2_ref, xs_ref, zero_scr, sem, zsem, *, td, slabs, n_experts):
    @pl.when(pl.program_id(0) == 0)
    def _():
        zero_scr[...] = jnp.zeros_like(zero_scr)
        n_blocks = xs_ref.shape[0] // (EXPERT_BLOCK * slabs)
        n_used = end_ref[n_experts - 1] // EXPERT_BLOCK

        def zero_block(blk):
            row0 = pl.multiple_of(blk * (EXPERT_BLOCK * slabs), EXPERT_BLOCK * slabs)
            return pltpu.make_async_copy(zero_scr, xs_ref.at[pl.ds(row0, EXPERT_BLOCK * slabs)], zsem)

        todo = [(cnt_ref[e] > 0, end_ref[e] // EXPERT_BLOCK - 1) for e in range(n_experts)]
        todo += [(blk >= n_used, blk) for blk in range(max(n_blocks - n_experts - EXPERT_PAIR + 1, 0), n_blocks)]
        for cond, blk in todo:
            @pl.when(cond)
            def _():
                zero_block(blk).start()
        for cond, blk in todo:
            @pl.when(cond)
            def _():
                zero_block(blk).wait()

    def issue(q, carry):
        for uu in range(DMA_UNROLL):
            t = q * DMA_UNROLL + uu
            for k in range(TOP_K):
                dst = pl.multiple_of(pos_ref[k, t] * slabs, slabs)
                pltpu.make_async_copy(u2_ref.at[pl.ds(t * slabs, slabs)], xs_ref.at[pl.ds(dst, slabs)],
                                      sem).start(priority=k)
        return carry

    lax.fori_loop(0, td // DMA_UNROLL, issue, 0)
    for _ in range(TOP_K):
        pltpu.make_async_copy(u2_ref, xs_ref.at[pl.ds(0, td * slabs)], sem).wait()


def _expert_kernel(be_ref, nu_ref, xs_ref, *rest, slabs):
    w_f32 = (rest[0:3], rest[3:6])
    ys_ref = rest[6]
    w_bf16 = (rest[7:10], rest[10:13])
    i = pl.program_id(0)
    half = slabs * LANES
    blk_rows = EXPERT_BLOCK * slabs

    for b in range(EXPERT_PAIR):
        j = EXPERT_PAIR * i + b

        @pl.when((i == 0) | (be_ref[j] != be_ref[jnp.maximum(j - EXPERT_PAIR, 0)]))
        def _():
            for dst, src in zip(w_bf16[b], w_f32[b]):
                dst[...] = src[...].astype(BF16)

    def ffn(b):
        wg_b, wu_b, wd_b = w_bf16[b]
        lo, hi = _unpack_bf16_pairs(_load_row_slabs(xs_ref, b * EXPERT_BLOCK, EXPERT_BLOCK, slabs))
        gate = _dot(lo, wg_b[0:half, :]) + _dot(hi, wg_b[half:, :])
        up = _dot(lo, wu_b[0:half, :]) + _dot(hi, wu_b[half:, :])
        act = (gate * _sigmoid(gate)) * up
        _store_row_slabs(ys_ref, _pack_bf16_pairs(_dot(act.astype(BF16), wd_b[...])), b * EXPERT_BLOCK)

    def zero(b):
        ys_ref[pl.ds(b * blk_rows, blk_rows), :] = jnp.zeros((blk_rows, LANES), U32)

    n_valid = nu_ref[0] - EXPERT_PAIR * i

    @pl.when(n_valid >= 2)
    def _():
        ffn(0)
        ffn(1)

    @pl.when(n_valid == 1)
    def _():
        ffn(0)
        zero(1)

    @pl.when(n_valid <= 0)
    def _():
        zero(0)
        zero(1)


def _combine_kernel(pos_cur_ref, pos_nxt_ref, hmid_ref, route_ref, p_ref, g3_ref, wple_ref, wpg_ref, gf_ref,
                    ys_ref, out_ref, ybuf0, ybuf1, sems, *, tp, slabs):
    i = pl.program_id(0)
    n = pl.num_programs(0)

    def gather_row(pos_ref, buf, s, t, k):
        src = pl.multiple_of(pos_ref[k, t] * slabs, slabs)
        return pltpu.make_async_copy(ys_ref.at[pl.ds(src, slabs)], buf.at[pl.ds((k * tp + t) * slabs, slabs)],
                                     sems.at[s])

    def wait_tile(buf, s):
        for k in range(TOP_K):
            pltpu.make_async_copy(ys_ref.at[pl.ds(0, tp * slabs)], buf.at[pl.ds(k * tp * slabs, tp * slabs)],
                                  sems.at[s]).wait()

    @pl.when(i == 0)
    def _():
        def body(q, carry):
            for uu in range(DMA_UNROLL):
                for k in range(TOP_K):
                    gather_row(pos_cur_ref, ybuf0, 0, q * DMA_UNROLL + uu, k).start(priority=k)
            return carry
        lax.fori_loop(0, tp // DMA_UNROLL, body, 0)

    def step(buf, s, nbuf, ns):
        wait_tile(buf, s)
        for t in range(tp):
            for k in range(TOP_K):
                gather_row(pos_nxt_ref, nbuf, ns, t, k).start(priority=k)
        h3 = _ple_update(hmid_ref, route_ref, p_ref, g3_ref, wple_ref, wpg_ref, buf, tp, slabs)
        out_ref[...] = _rms_norm(h3, gf_ref[...])

        @pl.when(i == n - 1)
        def _():
            wait_tile(nbuf, ns)

    @pl.when(lax.rem(i, 2) == 0)
    def _():
        step(ybuf0, 0, ybuf1, 1)

    @pl.when(lax.rem(i, 2) == 1)
    def _():
        step(ybuf1, 1, ybuf0, 0)


def _const_spec(shape, layer):
    nd = len(shape)
    return pl.BlockSpec((None,) + tuple(shape), lambda i: (layer,) + (0,) * nd, pipeline_mode=pl.Buffered(1))


def _block_diag_gates(w_r, w_i):
    n_layers, n_heads, hd, _ = w_r.shape
    per = LANES // hd
    n_slab = n_heads // per

    def bd(w):
        w = w.reshape(n_layers, n_slab, per, hd, hd)
        eye = jnp.eye(per, dtype=w.dtype)
        out = jnp.einsum('lspij,pq->lspiqj', w, eye)
        return out.reshape(n_layers, n_slab, LANES, LANES)

    return jnp.concatenate([bd(w_r), bd(w_i)], axis=-1).astype(BF16)


def _tile_sizes(seq, n_tok):
    mixer_rows = min(256, seq * SEQ_GROUP)
    rank_rows = min(2048, n_tok)
    combine_rows = min(256, n_tok)
    assert (seq * SEQ_GROUP) % mixer_rows == 0 and n_tok % rank_rows == 0 and rank_rows % combine_rows == 0
    assert mixer_rows % SEQ_GROUP == 0 and rank_rows % DMA_UNROLL == 0 and combine_rows % DMA_UNROLL == 0
    return mixer_rows, rank_rows, combine_rows


def kernel(x, p, norm_mix_g, w_in, conv_w, conv_b, w_rg, b_rg, w_ig, b_ig, lru_lambda, w_pool, pool_scale, w_up_a, w_up_b, w_out, norm_ffn_g, w_router_group, b_router_group, w_router_expert, b_router_expert, w_gate_e, w_up_e, w_down_e, norm_ple_g, w_ple, w_ple_gate, final_norm_g):
    bsz, seq, d_model = x.shape
    n_layers, _, d_in = w_in.shape
    d_lru = conv_w.shape[2]
    d_pool = pool_scale.shape[1]
    n_groups = w_router_group.shape[2]
    n_experts = w_router_expert.shape[2]
    experts_per_group = n_experts // n_groups
    d_expert = w_gate_e.shape[3]
    ple_dim = w_ple.shape[1]
    n_tok = bsz * seq
    half = d_model // 2
    x_slabs = half // LANES

    rows_m, tr, tp = _tile_sizes(seq, n_tok)
    assert tp == rows_m
    assert bsz % SEQ_GROUP == 0
    assert d_in == 2 * d_lru + d_pool + 2 * d_model
    assert LANES % w_rg.shape[2] == 0 and d_lru % LANES == 0 and d_model % (2 * LANES) == 0
    assert w_pool.shape[2] == LANES and (2 << (w_pool.shape[1] - 1)) <= HALO_STEPS and conv_w.shape[1] - 1 <= HALO_STEPS
    assert n_groups + n_experts <= LANES

    n_blocks = (n_tok * TOP_K + n_experts * (EXPERT_BLOCK - 1) + EXPERT_BLOCK - 1) // EXPERT_BLOCK
    n_blocks = -(-n_blocks // EXPERT_PAIR) * EXPERT_PAIR
    n_rows = n_blocks * EXPERT_BLOCK

    w_in_b = w_in.astype(BF16)
    w_gate_bd = _block_diag_gates(w_rg, w_ig)
    w_pool_b = w_pool.astype(BF16)
    w_up_a_b = w_up_a.astype(BF16)
    w_up_b_b = w_up_b.astype(BF16)
    w_out_b = w_out.astype(BF16)
    w_router = jnp.concatenate([w_router_group, w_router_expert], axis=-1)
    w_router = jnp.pad(w_router, ((0, 0), (0, 0), (0, LANES - w_router.shape[-1]))).astype(BF16)
    b_router = jnp.concatenate([b_router_group, b_router_expert], axis=-1)
    b_router = jnp.pad(b_router, ((0, 0), (0, LANES - b_router.shape[-1])))[:, None, :]
    w_ple_b = w_ple.astype(BF16)
    w_pg_b = w_ple_gate.astype(BF16)
    row3 = lambda a: a[:, None, :]
    g1, g2, g3 = row3(norm_mix_g), row3(norm_ffn_g), row3(norm_ple_g)
    conv_b3, b_rg3, b_ig3, lam3, pscale3 = row3(conv_b), row3(b_rg), row3(b_ig), row3(lru_lambda), row3(pool_scale)
    gf = final_norm_g[None, :]

    n_sg = bsz // SEQ_GROUP
    p2 =p.reshape(n_layers, n_sg, SEQ_GROUP, seq, ple_dim).transpose(0, 1, 3, 2, 4).reshape(n_layers, n_tok, ple_dim)
    n_chunks = seq * SEQ_GROUP // rows_m
    n_mt = n_tok // rows_m
    n_rt = n_tok // tr
    per_rt = tr // tp
    n_ct = n_tok // tp

    def pos_spec(shift, n_tiles):
        return pl.BlockSpec(
            (None, SUBLANES, tp),
            lambda i: (jnp.minimum(i + shift, n_tiles - 1) // per_rt, 0, jnp.minimum(i + shift, n_tiles - 1) % per_rt),
            memory_space=pltpu.SMEM)

    for layer in range(n_layers):
        cs = functools.partial(_const_spec, layer=layer)
        in_map = lambda s: (jnp.minimum(s, n_mt - 1), 0)
        out_map = lambda s: (jnp.maximum(s - 1, 0), 0)
        fused = layer > 0
        if fused:
            csp = functools.partial(_const_spec, layer=layer - 1)
            head_specs = [
                pos_spec(0, n_mt), pos_spec(1, n_mt),
                pl.BlockSpec((rows_m, d_model), in_map),
                pl.BlockSpec((rows_m, LANES), in_map),
                pl.BlockSpec((None, rows_m, ple_dim), lambda s, layer=layer: (layer - 1, jnp.minimum(s, n_mt - 1), 0)),
                csp((1, d_model)), csp((ple_dim, d_model)), csp((d_model, d_model)), cs((1, d_model)),
                pl.BlockSpec(memory_space=pl.ANY),
            ]
            head_args = (pos, pos, h_mid, route, p2, g3, w_ple_b, w_pg_b, g1, ys)
            head_scratch = [
                pltpu.VMEM((rows_m, d_model), F32), pltpu.VMEM((rows_m, d_model), F32),
                pltpu.VMEM((rows_m, d_model), BF16), pltpu.VMEM((rows_m, d_model), BF16),
                pltpu.VMEM((TOP_K * rows_m * x_slabs, LANES), U32), pltpu.VMEM((TOP_K * rows_m * x_slabs, LANES), U32),
                pltpu.SemaphoreType.DMA((2,)),
            ]
        else:
            def x_map(s):
                t = jnp.minimum(s, n_mt - 1)
                return (t // n_chunks, 0, t % n_chunks, 0)
            head_specs = [pl.BlockSpec((None, SEQ_GROUP, rows_m // SEQ_GROUP, d_model), x_map), cs((1, d_model))]
            head_args = (x.reshape(n_sg, SEQ_GROUP, seq, d_model), g1)
            head_scratch = [pltpu.VMEM((d_model // LANES, rows_m, LANES), F32),
                            pltpu.VMEM((d_model // LANES, rows_m, LANES), F32),
                            pltpu.VMEM((rows_m, d_model), BF16), pltpu.VMEM((rows_m, d_model), BF16)]
        h_mid, u2, route, counts = pl.pallas_call(
            functools.partial(_mixer_kernel, fused=fused, rows=rows_m, n_chunks=n_chunks, d_model=d_model,
                              d_lru=d_lru, d_pool=d_pool, n_groups=n_groups, experts_per_group=experts_per_group,
                              slabs=x_slabs),
            grid=(n_mt + 1,),
            in_specs=head_specs + [
                cs((d_model, d_in)), cs(conv_w.shape[1:]), cs((1, d_lru)),
                cs(w_gate_bd.shape[1:]), cs((1, d_lru)), cs((1, d_lru)), cs((1, d_lru)),
                cs(w_pool.shape[1:]), cs((1, d_pool)), cs((d_lru, d_model)), cs((d_pool, d_model)),
                cs((d_model, d_model)), cs((1, d_model)), cs((d_model, LANES)), cs((1, LANES)),
            ],
            out_specs=[
                pl.BlockSpec((rows_m, d_model), out_map),
                pl.BlockSpec((rows_m * x_slabs, LANES), out_map),
                pl.BlockSpec((rows_m, LANES), out_map),
                pl.BlockSpec((SUBLANES, LANES), lambda s: (0, 0)),
            ],
            out_shape=[
                jax.ShapeDtypeStruct((n_tok, d_model), F32),
                jax.ShapeDtypeStruct((n_tok * x_slabs, LANES), U32),
                jax.ShapeDtypeStruct((n_tok, LANES), F32),
                jax.ShapeDtypeStruct((SUBLANES, LANES), F32),
            ],
            scratch_shapes=[
                pltpu.VMEM((rows_m, d_in), F32), pltpu.VMEM((rows_m, d_in), F32),
                pltpu.VMEM((HALO + rows_m, d_lru), F32),
                pltpu.VMEM((HALO + rows_m, d_pool), F32),
                pltpu.VMEM((SEQ_GROUP, d_lru), F32),
                pltpu.VMEM((rows_m, d_lru), BF16),
                pltpu.VMEM((rows_m, d_pool), BF16),
                pltpu.VMEM((SUBLANES, LANES), F32),
            ] + head_scratch,
            compiler_params=pltpu.CompilerParams(dimension_semantics=("arbitrary",), vmem_limit_bytes=58 * MIB),
            name=f"mixer_l{layer}",
        )(*head_args, w_in_b, conv_w, conv_b3, w_gate_bd, b_rg3, b_ig3, lam3, w_pool_b, pscale3,
          w_up_a_b, w_up_b_b, w_out_b, g2, w_router, b_router)

        pos = pl.pallas_call(
            functools.partial(_rank_kernel, tr=tr, n_experts=n_experts),
            grid=(n_rt,),
            in_specs=[pl.BlockSpec((tr, LANES), lambda i: (i, 0)),
                      pl.BlockSpec((SUBLANES, LANES), lambda i: (0, 0))],
            out_specs=pl.BlockSpec((None, SUBLANES, tr), lambda i: (i, 0, 0)),
            out_shape=jax.ShapeDtypeStruct((n_rt, SUBLANES, tr), I32),
            scratch_shapes=[pltpu.VMEM((tr, tr), BF16), pltpu.VMEM((SUBLANES, LANES), F32),
                            pltpu.VMEM((SUBLANES, LANES), F32)],
            compiler_params=pltpu.CompilerParams(dimension_semantics=("arbitrary",)),
            name=f"rank_l{layer}",
        )(route, counts)

        cnt = counts[0, :n_experts].astype(I32)
        pad_end = jnp.cumsum(((cnt + EXPERT_BLOCK - 1) // EXPERT_BLOCK) * EXPERT_BLOCK)
        n_used = (pad_end[-1] // EXPERT_BLOCK).astype(I32)
        block_row0 = jnp.minimum(jnp.arange(n_blocks, dtype=I32), n_used - 1) * EXPERT_BLOCK
        block_expert = jnp.minimum(jnp.sum(pad_end[None, :] <= block_row0[:, None], axis=1), n_experts - 1).astype(I32)

        xs = pl.pallas_call(
            functools.partial(_dispatch_kernel, td=tr, slabs=x_slabs, n_experts=n_experts),
            grid_spec=pltpu.PrefetchScalarGridSpec(
                num_scalar_prefetch=2,
                grid=(n_rt,),
                in_specs=[
                    pl.BlockSpec((None, SUBLANES, tr), lambda i, cnt, end: (i, 0, 0), memory_space=pltpu.SMEM),
                    pl.BlockSpec((tr * x_slabs, LANES), lambda i, cnt, end: (i, 0)),
                ],
                out_specs=pl.BlockSpec(memory_space=pl.ANY),
                scratch_shapes=[pltpu.VMEM((EXPERT_BLOCK * x_slabs, LANES), U32),
                                pltpu.SemaphoreType.DMA, pltpu.SemaphoreType.DMA],
            ),
            out_shape=jax.ShapeDtypeStruct((n_rows * x_slabs, LANES), U32),
            compiler_params=pltpu.CompilerParams(dimension_semantics=("arbitrary",), has_side_effects=True),
            name=f"dispatch_l{layer}",
        )(cnt, pad_end, pos, u2)

        pair_rows = EXPERT_PAIR * EXPERT_BLOCK * x_slabs
        used_map = lambda i, be, nu: (jnp.minimum(i, (nu[0] - 1) // EXPERT_PAIR), 0)
        w_specs, w_scratch = [], []
        for b in range(EXPERT_PAIR):
            w_map = lambda i, be, nu, layer=layer, b=b: (layer, be[EXPERT_PAIR * i + b], 0, 0)
            for shape in ((d_model, d_expert), (d_model, d_expert), (d_expert, d_model)):
                w_specs.append(pl.BlockSpec((None, None) + shape, w_map))
                w_scratch.append(pltpu.VMEM(shape, BF16))
        ys = pl.pallas_call(
            functools.partial(_expert_kernel, slabs=x_slabs),
            grid_spec=pltpu.PrefetchScalarGridSpec(
                num_scalar_prefetch=2,
                grid=(n_blocks // EXPERT_PAIR,),
                in_specs=[pl.BlockSpec((pair_rows, LANES), used_map)] + w_specs,
                out_specs=pl.BlockSpec((pair_rows, LANES), lambda i, be, nu: (i, 0)),
                scratch_shapes=w_scratch,
            ),
            out_shape=jax.ShapeDtypeStruct((n_rows * x_slabs, LANES), U32),
            compiler_params=pltpu.CompilerParams(dimension_semantics=("arbitrary",), vmem_limit_bytes=48 * MIB),
            name=f"experts_l{layer}",
        )(block_expert, n_used[None], xs, *([w_gate_e, w_up_e, w_down_e] * EXPERT_PAIR))

    last = n_layers - 1
    csl = functools.partial(_const_spec, layer=last)
    tok_spec = pl.BlockSpec((tp, d_model), lambda i: (i, 0))
    h = pl.pallas_call(
        functools.partial(_combine_kernel, tp=tp, slabs=x_slabs),
        grid=(n_ct,),
        in_specs=[
            pos_spec(0, n_ct), pos_spec(1, n_ct),
            tok_spec,
            pl.BlockSpec((tp, LANES), lambda i: (i, 0)),
            pl.BlockSpec((None, tp, ple_dim), lambda i: (last, i, 0)),
            csl((1, d_model)), csl((ple_dim, d_model)), csl((d_model, d_model)),
            pl.BlockSpec((1, d_model), lambda i: (0, 0)),
            pl.BlockSpec(memory_space=pl.ANY),
        ],
        out_specs=tok_spec,
        out_shape=jax.ShapeDtypeStruct((n_tok, d_model), F32),
        scratch_shapes=[pltpu.VMEM((TOP_K * tp * x_slabs, LANES), U32),
                        pltpu.VMEM((TOP_K * tp * x_slabs, LANES), U32),
                        pltpu.SemaphoreType.DMA((2,))],
        compiler_params=pltpu.CompilerParams(dimension_semantics=("arbitrary",)),
        name="combine_final",
    )(pos, pos, h_mid, route, p2, g3, w_ple_b, w_pg_b, gf, ys)

    out = h.reshape(n_sg, seq, SEQ_GROUP, d_model).transpose(0, 2, 1, 3)
    return out.reshape(bsz, seq, d_model)
```
